```python
import math
import jax
import jax.numpy as jnp
from jax import lax
import numpy as np

D_MODEL = 4096
BATCH = 4
SEQ = 4096
DEPTH = 1

CHUNK = 64
N_MEM = 256

SSD_D_INNER = D_MODEL // 2
SSD_HEAD_DIM = 64
SSD_N_HEADS = SSD_D_INNER // SSD_HEAD_DIM
SSD_N_GROUPS = 8
SSD_D_STATE = 128
SSD_CONV = 4
SSD_CHUNK = CHUNK
SSD_XBC = SSD_D_INNER + 2 * SSD_N_GROUPS * SSD_D_STATE
SSD_DT_MIN = 0.001
SSD_DT_MAX = 0.1

LRU_WIDTH = D_MODEL // 2
LRU_BLOCKS = 16
LRU_BLOCK = LRU_WIDTH // LRU_BLOCKS
LRU_CONV = 4
LRU_C = 8.0

MEM_HEADS = 4
MEM_WIDTH = D_MODEL // 2
MEM_HEAD_DIM = MEM_WIDTH // MEM_HEADS

N_BRANCH = 3

PEER_HEADS = 8
PEER_N_KEYS = 128
PEER_N_EXPERTS = PEER_N_KEYS * PEER_N_KEYS
PEER_KEY_DIM = 256
PEER_HALF = PEER_KEY_DIM // 2
PEER_TOPK = 16
PEER_TOKEN_BLOCK = 128

SPLIT_SIZES = (SSD_D_INNER, SSD_XBC, SSD_N_HEADS, LRU_WIDTH, LRU_WIDTH, MEM_WIDTH, N_BRANCH * D_MODEL)
IN_COLS = sum(SPLIT_SIZES)
SPLIT_POINTS = [sum(SPLIT_SIZES[:i + 1]) for i in range(len(SPLIT_SIZES) - 1)]

DEEPNORM_ALPHA = (2.0 * DEPTH) ** 0.25
DEEPNORM_BETA = (8.0 * DEPTH) ** -0.25
LN_EPS = 1e-5
RMS_EPS = 1e-5

kernel_name = 'hybrid_ssd_rglru_memxattn_peer_deepnorm'


def _layer_norm(x, g, b):
    xf = x.astype(jnp.float32)
    mu = jnp.mean(xf, axis=-1, keepdims=True)
    var = jnp.mean(jnp.square(xf - mu), axis=-1, keepdims=True)
    y = (xf - mu) * lax.rsqrt(var + LN_EPS)
    return (y * g.astype(jnp.float32) + b.astype(jnp.float32)).astype(x.dtype)


def _causal_dwconv(x, w, b):
    k = w.shape[0]
    y = lax.conv_general_dilated(
        x, w[:, None, :].astype(x.dtype), window_strides=(1,), padding=[(k - 1, 0)],
        dimension_numbers=('NWC', 'WIO', 'NWC'), feature_group_count=x.shape[-1])
    return y + b.astype(x.dtype)


def _ssd_branch(z, xbc, dt, conv_w, conv_b, dt_bias, a_log, d_skip, norm_w):
    f32 = jnp.float32
    bsz, seq, _ = z.shape
    nc = seq // SSD_CHUNK
    hpg = SSD_N_HEADS // SSD_N_GROUPS
    gn = SSD_N_GROUPS * SSD_D_STATE
    xbc = jax.nn.silu(_causal_dwconv(xbc, conv_w, conv_b).astype(f32))
    xs, bm, cm = jnp.split(xbc, [SSD_D_INNER, SSD_D_INNER + gn], axis=-1)
    xs = xs.reshape(bsz, nc, SSD_CHUNK, SSD_N_GROUPS, hpg, SSD_HEAD_DIM)
    bm = bm.reshape(bsz, nc, SSD_CHUNK, SSD_N_GROUPS, SSD_D_STATE)
    cm = cm.reshape(bsz, nc, SSD_CHUNK, SSD_N_GROUPS, SSD_D_STATE)
    dt = jax.nn.softplus(dt.astype(f32) + dt_bias.astype(f32))
    a = -jnp.exp(a_log.astype(f32))
    dt = dt.reshape(bsz, nc, SSD_CHUNK, SSD_N_GROUPS, hpg)
    da = dt * a.reshape(SSD_N_GROUPS, hpg)
    xdt = xs * dt[..., None]
    cs = jnp.cumsum(da, axis=2)
    seg = cs[:, :, :, None] - cs[:, :, None, :]
    causal = jnp.tril(jnp.ones((SSD_CHUNK, SSD_CHUNK), dtype=bool))[:, :, None, None]
    decay = jnp.where(causal, jnp.exp(jnp.where(causal, seg, 0.0)), 0.0)
    cb = jnp.einsum('bclgn,bcsgn->bclsg', cm, bm)
    y_diag = jnp.einsum('bclsgr,bcsgrp->bclgrp', cb[..., None] * decay, xdt)
    decay_to_end = jnp.exp(cs[:, :, -1:] - cs)
    chunk_states = jnp.einsum('bclgn,bclgr,bclgrp->bcgrpn', bm, decay_to_end, xdt)
    chunk_decay = jnp.exp(cs[:, :, -1])

    def step(h, inp):
        st, dec = inp
        return h * dec[..., None, None] + st, h

    h0 = jnp.zeros((bsz, SSD_N_GROUPS, hpg, SSD_HEAD_DIM, SSD_D_STATE), f32)
    _, h_prev = lax.scan(step, h0, (jnp.moveaxis(chunk_states, 1, 0), jnp.moveaxis(chunk_decay, 1, 0)))
    h_prev = jnp.moveaxis(h_prev, 0, 1)
    y_off = jnp.einsum('bclgn,bcgrpn,bclgr->bclgrp', cm, h_prev, jnp.exp(cs))
    y = y_diag + y_off + xs * d_skip.astype(f32).reshape(SSD_N_GROUPS, hpg)[:, :, None]
    y = y.reshape(bsz, seq, SSD_D_INNER)
    y = y * jax.nn.silu(z.astype(f32))
    yg = y.reshape(bsz, seq, SSD_N_GROUPS, SSD_D_INNER // SSD_N_GROUPS)
    yg = yg * lax.rsqrt(jnp.mean(jnp.square(yg), axis=-1, keepdims=True) + RMS_EPS)
    y = yg.reshape(bsz, seq, SSD_D_INNER) * norm_w.astype(f32)
    return y.astype(z.dtype)


def _rglru_branch(gate_in, x_in, conv_w, conv_b, w_a, b_a, w_i, b_i, lam):
    f32 = jnp.float32
    bsz, seq, _ = x_in.shape
    gate = jax.nn.gelu(gate_in.astype(f32), approximate=True)
    xc = _causal_dwconv(x_in, conv_w, conv_b).astype(f32)
    xb = xc.reshape(bsz, seq, LRU_BLOCKS, LRU_BLOCK)
    r = jax.nn.sigmoid(jnp.einsum('bshi,hij->bshj', xb, w_a.astype(f32)).reshape(bsz, seq, LRU_WIDTH) + b_a.astype(f32))
    i = jax.nn.sigmoid(jnp.einsum('bshi,hij->bshj', xb, w_i.astype(f32)).reshape(bsz, seq, LRU_WIDTH) + b_i.astype(f32))
    log_a = -LRU_C * r * jax.nn.softplus(-lam.astype(f32))
    a = jnp.exp(log_a)
    u = jnp.sqrt(-jnp.expm1(2.0 * log_a)) * (i * xc)

    def combine(lhs, rhs):
        a1, b1 = lhs
        a2, b2 = rhs
        return a1 * a2, a2 * b1 + b2

    _, h = lax.associative_scan(combine, (a, u), axis=1)
    return (h * gate).astype(x_in.dtype)


def _memory_xattn(q_in, mem, w_kv):
    bsz, seq, _ = q_in.shape
    n_mem = mem.shape[1]
    kv = mem @ w_kv
    k, v = jnp.split(kv, 2, axis=-1)
    q = q_in.reshape(bsz, seq, MEM_HEADS, MEM_HEAD_DIM)
    k = k.reshape(bsz, n_mem, MEM_HEADS, MEM_HEAD_DIM)
    v = v.reshape(bsz, n_mem, MEM_HEADS, MEM_HEAD_DIM)
    s = jnp.einsum('bshd,bmhd->bhsm', q, k).astype(jnp.float32) * (MEM_HEAD_DIM ** -0.5)
    p = jax.nn.softmax(s, axis=-1).astype(v.dtype)
    o = jnp.einsum('bhsm,bmhd->bshd', p, v)
    return o.reshape(bsz, seq, MEM_WIDTH)


def _peer(x, w_q, sub_keys, u_tab, v_tab):
    f32 = jnp.float32
    bsz, seq, d = x.shape
    t = bsz * seq
    xf = x.reshape(t, d)
    q = (xf @ w_q).astype(f32).reshape(t, PEER_HEADS, 2, PEER_HALF)
    s = jnp.einsum('thcd,hckd->thck', q, sub_keys.astype(f32))
    top_s, top_i = lax.top_k(s, PEER_TOPK)
    cand_s = top_s[:, :, 0, :, None] + top_s[:, :, 1, None, :]
    cand_i = top_i[:, :, 0, :, None] * PEER_N_KEYS + top_i[:, :, 1, None, :]
    best_s, best_pos = lax.top_k(cand_s.reshape(t, PEER_HEADS, PEER_TOPK * PEER_TOPK), PEER_TOPK)
    expert_idx = jnp.take_along_axis(cand_i.reshape(t, PEER_HEADS, PEER_TOPK * PEER_TOPK), best_pos, axis=-1)
    gates = jax.nn.softmax(best_s, axis=-1)
    nb = t // PEER_TOKEN_BLOCK

    def block(args):
        xb, idx, g = args
        u = u_tab[idx]
        v = v_tab[idx]
        act = jax.nn.gelu(jnp.einsum('thkd,td->thk', u, xb).astype(f32), approximate=False)
        return jnp.einsum('thk,thkd->td', (g * act).astype(v.dtype), v)

    out = lax.map(block, (xf.reshape(nb, PEER_TOKEN_BLOCK, d),
                          expert_idx.reshape(nb, PEER_TOKEN_BLOCK, PEER_HEADS, PEER_TOPK),
                          gates.reshape(nb, PEER_TOKEN_BLOCK, PEER_HEADS, PEER_TOPK)))
    return out.reshape(bsz, seq, d).astype(x.dtype)


def setup_inputs(seed: int = 0) -> dict:
    key = jax.random.key(seed)
    ks = jax.random.split(key, 40)
    L = DEPTH
    f32 = jnp.float32

    def nrm(k, shape, scale):
        return jax.random.normal(k, shape, f32) * scale

    x = nrm(ks[0], (BATCH, SEQ, D_MODEL), 1.0)
    mem = nrm(ks[1], (BATCH, N_MEM, D_MODEL), 1.0)
    w_in = nrm(ks[2], (L, D_MODEL, IN_COLS), D_MODEL ** -0.5)
    b_gate = nrm(ks[3], (L, N_BRANCH, D_MODEL), 0.02)
    ssd_conv_w = nrm(ks[4], (L, SSD_CONV, SSD_XBC), SSD_CONV ** -0.5)
    ssd_conv_b = nrm(ks[5], (L, SSD_XBC), 0.02)
    u_dt = jax.random.uniform(ks[6], (L, SSD_N_HEADS), f32)
    dt0 = jnp.exp(u_dt * (math.log(SSD_DT_MAX) - math.log(SSD_DT_MIN)) + math.log(SSD_DT_MIN))
    ssd_dt_bias = dt0 + jnp.log(-jnp.expm1(-dt0))
    ssd_a_log = jnp.log(jax.random.uniform(ks[7], (L, SSD_N_HEADS), f32, minval=1.0, maxval=16.0))
    ssd_d = 1.0 + nrm(ks[8], (L, SSD_N_HEADS), 0.1)
    ssd_norm_w = 1.0 + nrm(ks[9], (L, SSD_D_INNER), 0.02)
    lru_conv_w = nrm(ks[10], (L, LRU_CONV, LRU_WIDTH), LRU_CONV ** -0.5)
    lru_conv_b = nrm(ks[11], (L, LRU_WIDTH), 0.02)
    lru_w_a = nrm(ks[12], (L, LRU_BLOCKS, LRU_BLOCK, LRU_BLOCK), LRU_BLOCK ** -0.5)
    lru_b_a = nrm(ks[13], (L, LRU_WIDTH), 0.02)
    lru_w_i = nrm(ks[14], (L, LRU_BLOCKS, LRU_BLOCK, LRU_BLOCK), LRU_BLOCK ** -0.5)
    lru_b_i = nrm(ks[15], (L, LRU_WIDTH), 0.02)
    a_pow_c = jax.random.uniform(ks[16], (L, LRU_WIDTH), f32, minval=0.9, maxval=0.999)
    log_a0 = jnp.log(a_pow_c) / LRU_C
    lru_lambda = log_a0 - jnp.log(-jnp.expm1(log_a0))
    mem_w_kv = jnp.concatenate([nrm(ks[17], (L, D_MODEL, MEM_WIDTH), D_MODEL ** -0.5),
                                nrm(ks[18], (L, D_MODEL, MEM_WIDTH), D_MODEL ** -0.5 * DEEPNORM_BETA)], axis=-1)
    w_branch_ssd = nrm(ks[19], (L, SSD_D_INNER, D_MODEL), SSD_D_INNER ** -0.5 * DEEPNORM_BETA)
    w_branch_lru = nrm(ks[20], (L, LRU_WIDTH, D_MODEL), LRU_WIDTH ** -0.5 * DEEPNORM_BETA)
    w_branch_mem = nrm(ks[21], (L, MEM_WIDTH, D_MODEL), MEM_WIDTH ** -0.5 * DEEPNORM_BETA)
    w_out = nrm(ks[22], (L, D_MODEL, D_MODEL), D_MODEL ** -0.5 * DEEPNORM_BETA)
    ln1_g = 1.0 + nrm(ks[23], (L, D_MODEL), 0.02)
    ln1_b = nrm(ks[24], (L, D_MODEL), 0.02)
    peer_w_q = nrm(ks[25], (L, D_MODEL, PEER_HEADS * PEER_KEY_DIM), D_MODEL ** -0.5)
    peer_keys = nrm(ks[26], (L, PEER_HEADS, 2, PEER_N_KEYS, PEER_HALF), PEER_HALF ** -0.5)
    peer_u = nrm(ks[27], (L, PEER_N_EXPERTS, D_MODEL), D_MODEL ** -0.5)
    peer_v = nrm(ks[28], (L, PEER_N_EXPERTS, D_MODEL), DEEPNORM_BETA)
    ln2_g = 1.0 + nrm(ks[29], (L, D_MODEL), 0.02)
    ln2_b = nrm(ks[30], (L, D_MODEL), 0.02)
    return {'x': x, 'mem': mem, 'w_in': w_in, 'b_gate': b_gate,
            'ssd_conv_w': ssd_conv_w, 'ssd_conv_b': ssd_conv_b, 'ssd_dt_bias': ssd_dt_bias,
            'ssd_a_log': ssd_a_log, 'ssd_d': ssd_d, 'ssd_norm_w': ssd_norm_w,
            'lru_conv_w': lru_conv_w, 'lru_conv_b': lru_conv_b, 'lru_w_a': lru_w_a, 'lru_b_a': lru_b_a,
            'lru_w_i': lru_w_i, 'lru_b_i': lru_b_i, 'lru_lambda': lru_lambda,
            'mem_w_kv': mem_w_kv, 'w_branch_ssd': w_branch_ssd, 'w_branch_lru': w_branch_lru,
            'w_branch_mem': w_branch_mem, 'w_out': w_out, 'ln1_g': ln1_g, 'ln1_b': ln1_b,
            'peer_w_q': peer_w_q, 'peer_keys': peer_keys, 'peer_u': peer_u, 'peer_v': peer_v,
            'ln2_g': ln2_g, 'ln2_b': ln2_b}


def reference(x, mem, w_in, b_gate, ssd_conv_w, ssd_conv_b, ssd_dt_bias, ssd_a_log, ssd_d, ssd_norm_w,
              lru_conv_w, lru_conv_b, lru_w_a, lru_b_a, lru_w_i, lru_b_i, lru_lambda,
              mem_w_kv, w_branch_ssd, w_branch_lru, w_branch_mem, w_out, ln1_g, ln1_b,
              peer_w_q, peer_keys, peer_u, peer_v, ln2_g, ln2_b):
    bsz, seq, d = x.shape
    h = x
    for l in range(DEPTH):
        proj = h @ w_in[l]
        z, xbc, dt, lru_g, lru_x, mem_q, gate_pre = jnp.split(proj, SPLIT_POINTS, axis=-1)
        y_ssd = _ssd_branch(z, xbc, dt, ssd_conv_w[l], ssd_conv_b[l], ssd_dt_bias[l], ssd_a_log[l], ssd_d[l], ssd_norm_w[l])
        y_lru = _rglru_branch(lru_g, lru_x, lru_conv_w[l], lru_conv_b[l], lru_w_a[l], lru_b_a[l], lru_w_i[l], lru_b_i[l], lru_lambda[l])
        y_mem = _memory_xattn(mem_q, mem, mem_w_kv[l])
        gates = jax.nn.sigmoid((gate_pre.reshape(bsz, seq, N_BRANCH, d) + b_gate[l]).astype(jnp.float32)).astype(h.dtype)
        merged = (gates[:, :, 0] * (y_ssd @ w_branch_ssd[l])
                  + gates[:, :, 1] * (y_lru @ w_branch_lru[l])
                  + gates[:, :, 2] * (y_mem @ w_branch_mem[l]))
        h = _layer_norm(DEEPNORM_ALPHA * h + merged @ w_out[l], ln1_g[l], ln1_b[l])
        h = _layer_norm(DEEPNORM_ALPHA * h + _peer(h, peer_w_q[l], peer_keys[l], peer_u[l], peer_v[l]), ln2_g[l], ln2_b[l])
    return h
```

```python
import functools
import math

import jax
import jax.numpy as jnp
from jax import lax
from jax.experimental import pallas as pl
from jax.experimental.pallas import tpu as pltpu

F32 = jnp.float32
BF16 = jnp.bfloat16
I32 = jnp.int32

SSD_HEAD_DIM = 64
SSD_D_STATE = 128
SSD_CONV = 4
LRU_CONV = 4
LRU_C = 8.0
MEM_HEADS = 4
PEER_TOPK = 16
LN_EPS = 1e-5
RMS_EPS = 1e-5

LANES = 128
SUBLANES = 8
VMEM_LIMIT_BYTES = 56 * 1024 * 1024

HIGHEST = lax.Precision.HIGHEST
NEG_INF = float("-inf")
ID_SENTINEL = 1e9


def _params(*semantics):
    return pltpu.CompilerParams(dimension_semantics=semantics, vmem_limit_bytes=VMEM_LIMIT_BYTES)


def _tile(dim, want, align=LANES):
    if dim <= want:
        return dim
    t = want - want % align
    while t > align and dim % t:
        t -= align
    assert dim % t == 0, (dim, want)
    return t


def _sigmoid(v):
    return 1.0 / (1.0 + jnp.exp(-v))


def _softplus(v):
    return jnp.maximum(v, 0.0) + jnp.log1p(jnp.exp(-jnp.abs(v)))


def _matmul_kernel(*refs, nk, res_scale, has_res):
    a_ref, b_ref = refs[0], refs[1]
    res_ref = refs[2] if has_res else None
    o_ref = refs[3] if has_res else refs[2]
    acc_ref = refs[-1] if nk > 1 else None

    def finish(acc):
        if has_res:
            acc = acc + res_scale * res_ref[...].astype(F32)
        o_ref[...] = acc.astype(o_ref.dtype)

    part = jnp.dot(a_ref[...], b_ref[...], preferred_element_type=F32)
    if nk == 1:
        finish(part)
        return
    k = pl.program_id(2)

    @pl.when(k == 0)
    def _():
        acc_ref[...] = part

    @pl.when(k > 0)
    def _():
        acc_ref[...] += part

    @pl.when(k == nk - 1)
    def _():
        finish(acc_ref[...])


def _matmul(a, b, out_dtype, *, tm=1024, tn=1024, tk=None, res=None, res_scale=1.0, name="matmul"):
    m, kdim = a.shape
    _, n = b.shape
    tm, tn = _tile(m, tm), _tile(n, tn)
    tk = kdim if tk is None else _tile(kdim, tk)
    nk = kdim // tk
    in_specs = [pl.BlockSpec((tm, tk), lambda i, j, k: (i, k)),
                pl.BlockSpec((tk, tn), lambda i, j, k: (k, j))]
    args = [a, b]
    if res is not None:
        in_specs.append(pl.BlockSpec((tm, tn), lambda i, j, k: (i, j)))
        args.append(res)
    return pl.pallas_call(
        functools.partial(_matmul_kernel, nk=nk, res_scale=res_scale, has_res=res is not None),
        grid=(m // tm, n // tn, nk),
        in_specs=in_specs,
        out_specs=pl.BlockSpec((tm, tn), lambda i, j, k: (i, j)),
        out_shape=jax.ShapeDtypeStruct((m, n), out_dtype),
        scratch_shapes=[pltpu.VMEM((tm, tn), F32)] if nk > 1 else [],
        compiler_params=_params("parallel", "parallel", "arbitrary"),
        name=name,
    )(*args)


def _causal_conv(x, xpad_ref, w_ref, b_ref, first, n_taps):
    rows = x.shape[0]

    @pl.when(first)
    def _():
        xpad_ref[0:SUBLANES, :] = jnp.zeros((SUBLANES, x.shape[1]), F32)

    xpad_ref[SUBLANES:SUBLANES + rows, :] = x
    w = w_ref[...]
    y = b_ref[...] + w[n_taps - 1:n_taps, :] * x
    for back in range(1, n_taps):
        tap = n_taps - 1 - back
        y = y + w[tap:tap + 1, :] * xpad_ref[SUBLANES - back:SUBLANES - back + rows, :]
    xpad_ref[0:SUBLANES, :] = x[rows - SUBLANES:rows, :]
    return y


def _ssd_kernel(z_ref, xbc_ref, dt_ref, cw_ref, cb_ref, dtb_ref, alog_ref, dskip_ref, nw_ref, e_ref,
                y_ref, xpad_ref, state_ref, *, n_groups, heads_per_group):
    rows = z_ref.shape[0]
    d_inner = z_ref.shape[1]
    gn = n_groups * SSD_D_STATE
    gw = heads_per_group * SSD_HEAD_DIM
    first = pl.program_id(1) == 0

    @pl.when(first)
    def _():
        state_ref[...] = jnp.zeros(state_ref.shape, F32)

    conv = _causal_conv(xbc_ref[...].astype(F32), xpad_ref, cw_ref, cb_ref, first, SSD_CONV)
    act = conv * _sigmoid(conv)

    dtv = _softplus(dt_ref[...] + dtb_ref[...])
    da = dtv * (-jnp.exp(alog_ref[...]))
    r_i = lax.broadcasted_iota(I32, (rows, rows), 0)
    c_i = lax.broadcasted_iota(I32, (rows, rows), 1)
    causal = r_i >= c_i
    cs = jnp.dot(causal.astype(F32), da, precision=HIGHEST, preferred_element_type=F32)
    cs_end = cs[rows - 1:rows, :]
    cs_t = cs.T

    stack = jnp.concatenate([dtv, jnp.exp(cs_end - cs), jnp.exp(cs)], axis=0)
    ex = jnp.dot(stack, e_ref[...], precision=HIGHEST, preferred_element_type=F32)
    dt_x, dte_x, ecs_x = ex[0:rows], ex[rows:2 * rows], ex[2 * rows:3 * rows]

    xs = act[:, 0:d_inner]
    xdt = xs * dt_x
    xdt_end = xdt * dte_x
    lane = lax.broadcasted_iota(I32, (1, gw), 1)

    for g in range(n_groups):
        lo, hi = g * gw, (g + 1) * gw
        b_f = act[:, d_inner + g * SSD_D_STATE:d_inner + (g + 1) * SSD_D_STATE]
        c_g = act[:, d_inner + gn + g * SSD_D_STATE:d_inner + gn + (g + 1) * SSD_D_STATE].astype(BF16)
        b_g = b_f.astype(BF16)
        cb = lax.dot_general(c_g, b_g, (((1,), (1,)), ((), ())), preferred_element_type=F32)
        st = state_ref[g]
        y_off = jnp.dot(c_g, st.astype(BF16), preferred_element_type=F32) * ecs_x[:, lo:hi]
        xdt_g = xdt[:, lo:hi]
        y_diag = jnp.zeros((rows, gw), F32)
        for r in range(heads_per_group):
            h = g * heads_per_group + r
            seg = cs[:, h:h + 1] - cs_t[h:h + 1, :]
            decay = jnp.where(causal, jnp.exp(jnp.where(causal, seg, 0.0)), 0.0)
            m_h = (cb * decay).astype(BF16)
            in_head = (lane >= r * SSD_HEAD_DIM) & (lane < (r + 1) * SSD_HEAD_DIM)
            x_h = jnp.where(in_head, xdt_g, 0.0).astype(BF16)
            y_diag = y_diag + jnp.dot(m_h, x_h, preferred_element_type=F32)
        new_st = st * ecs_x[rows - 1:rows, lo:hi] + jnp.dot(
            b_f.T.astype(BF16), xdt_end[:, lo:hi].astype(BF16), preferred_element_type=F32)
        state_ref[g] = new_st

        y = y_diag + y_off + xs[:, lo:hi] * dskip_ref[:, lo:hi]
        zg = z_ref[:, lo:hi].astype(F32)
        y = y * (zg * _sigmoid(zg))
        ms = jnp.mean(y * y, axis=-1, keepdims=True)
        y = y * lax.rsqrt(ms + RMS_EPS) * nw_ref[:, lo:hi]
        y_ref[:, lo:hi] = y.astype(y_ref.dtype)


def _ssd_branch(xz, dt_pad, conv_w, conv_b, dt_bias, a_log, d_skip, norm_w, *, batch, seq, chunk=128):
    t = xz.shape[0]
    xbc_w = conv_w.shape[1]
    d_inner = xz.shape[1] - xbc_w
    assert xbc_w % d_inner == 0
    n_heads = dt_bias.shape[0]
    n_groups = (xbc_w - d_inner) // (2 * SSD_D_STATE)
    hpg = n_heads // n_groups
    assert d_inner == n_heads * SSD_HEAD_DIM and d_inner // n_groups == hpg * SSD_HEAD_DIM
    assert n_heads <= LANES
    chunk = _tile(seq, chunk)
    nc = seq // chunk
    pad = LANES - n_heads
    dtb = jnp.pad(dt_bias.astype(F32), (0, pad)).reshape(1, LANES)
    alog = jnp.pad(a_log.astype(F32), (0, pad)).reshape(1, LANES)
    dskip = jnp.repeat(d_skip.astype(F32), SSD_HEAD_DIM).reshape(1, d_inner)
    expand = (jnp.arange(LANES)[:, None] == (jnp.arange(d_inner)[None, :] // SSD_HEAD_DIM)).astype(F32)
    row = lambda b, c: (b * nc + c, 0)
    fixed = lambda b, c: (0, 0)
    return pl.pallas_call(
        functools.partial(_ssd_kernel, n_groups=n_groups, heads_per_group=hpg),
        grid=(batch, nc),
        in_specs=[pl.BlockSpec((chunk, d_inner), lambda b, c: (b * nc + c, xbc_w // d_inner)),
                  pl.BlockSpec((chunk, xbc_w), row),
                  pl.BlockSpec((chunk, LANES), row),
                  pl.BlockSpec((SSD_CONV, xbc_w), fixed),
                  pl.BlockSpec((1, xbc_w), fixed),
                  pl.BlockSpec((1, LANES), fixed),
                  pl.BlockSpec((1, LANES), fixed),
                  pl.BlockSpec((1, d_inner), fixed),
                  pl.BlockSpec((1, d_inner), fixed),
                  pl.BlockSpec((LANES, d_inner), fixed)],
        out_specs=pl.BlockSpec((chunk, d_inner), row),
        out_shape=jax.ShapeDtypeStruct((t, d_inner), BF16),
        scratch_shapes=[pltpu.VMEM((chunk + SUBLANES, xbc_w), F32),
                        pltpu.VMEM((n_groups, SSD_D_STATE, hpg * SSD_HEAD_DIM), F32)],
        compiler_params=_params("arbitrary", "arbitrary"),
        name="ssd_scan",
    )(xz, xz, dt_pad, conv_w.astype(F32), conv_b.astype(F32).reshape(1, xbc_w), dtb, alog, dskip,
      norm_w.astype(F32).reshape(1, d_inner), expand)


def _rglru_kernel(g_ref, x_ref, cw_ref, cb_ref, wa_ref, ba_ref, wi_ref, bi_ref, lam_ref,
                  y_ref, xpad_ref, carry_ref):
    rows, width = x_ref.shape
    n_blk = wa_ref.shape[0]
    blk = wa_ref.shape[1]
    first = pl.program_id(2) == 0

    @pl.when(first)
    def _():
        carry_ref[...] = jnp.zeros(carry_ref.shape, F32)

    xc = _causal_conv(x_ref[...].astype(F32), xpad_ref, cw_ref, cb_ref, first, LRU_CONV)
    xcb = xc.astype(BF16)
    ra = jnp.concatenate([jnp.dot(xcb[:, k * blk:(k + 1) * blk], wa_ref[k], preferred_element_type=F32)
                          for k in range(n_blk)], axis=1)
    ri = jnp.concatenate([jnp.dot(xcb[:, k * blk:(k + 1) * blk], wi_ref[k], preferred_element_type=F32)
                          for k in range(n_blk)], axis=1)
    r = _sigmoid(ra + ba_ref[...])
    i = _sigmoid(ri + bi_ref[...])
    log_a = (-LRU_C) * r * _softplus(-lam_ref[...])
    a = jnp.exp(log_a)
    th = jnp.tanh(log_a)
    u = jnp.sqrt(-2.0 * th / (1.0 - th)) * (i * xc)

    row_id = lax.broadcasted_iota(I32, (rows, width), 0)
    shift = 1
    while shift < rows:
        keep = row_id >= shift
        a_prev = jnp.where(keep, pltpu.roll(a, shift, 0), 1.0)
        u_prev = jnp.where(keep, pltpu.roll(u, shift, 0), 0.0)
        u = a * u_prev + u
        a = a * a_prev
        shift *= 2
    h = u + a * carry_ref[0:1, :]
    carry_ref[...] = jnp.broadcast_to(h[rows - 1:rows, :], carry_ref.shape)

    gv = g_ref[...].astype(F32)
    gelu = 0.5 * gv * (1.0 + jnp.tanh(math.sqrt(2.0 / math.pi) * (gv + 0.044715 * (gv * gv * gv))))
    y_ref[...] = (h * gelu).astype(y_ref.dtype)


def _rglru_branch(gx, conv_w, conv_b, w_a, b_a, w_i, b_i, lam, *, batch, seq, chunk=256, cw=512):
    t = gx.shape[0]
    width = conv_w.shape[1]
    n_blocks, blk, _ = w_a.shape
    cw = _tile(width, cw)
    assert cw % blk == 0
    bpc = cw // blk
    chunk = _tile(seq, chunk)
    nc = seq // chunk
    row = lambda b, j, c: (b * nc + c, j)
    col = lambda b, j, c: (0, j)
    wblk = lambda b, j, c: (j, 0, 0)
    vec = lambda v: v.astype(F32).reshape(1, width)
    return pl.pallas_call(
        _rglru_kernel,
        grid=(batch, width // cw, nc),
        in_specs=[pl.BlockSpec((chunk, cw), row),
                  pl.BlockSpec((chunk, cw), lambda b, j, c: (b * nc + c, width // cw + j)),
                  pl.BlockSpec((LRU_CONV, cw), col),
                  pl.BlockSpec((1, cw), col),
                  pl.BlockSpec((bpc, blk, blk), wblk),
                  pl.BlockSpec((1, cw), col),
                  pl.BlockSpec((bpc, blk, blk), wblk),
                  pl.BlockSpec((1, cw), col),
                  pl.BlockSpec((1, cw), col)],
        out_specs=pl.BlockSpec((chunk, cw), row),
        out_shape=jax.ShapeDtypeStruct((t, width), BF16),
        scratch_shapes=[pltpu.VMEM((chunk + SUBLANES, cw), F32), pltpu.VMEM((SUBLANES, cw), F32)],
        compiler_params=_params("arbitrary", "arbitrary", "arbitrary"),
        name="rglru_scan",
    )(gx, gx, conv_w.astype(F32), vec(conv_b), w_a.astype(BF16), vec(b_a), w_i.astype(BF16), vec(b_i),
      vec(lam))


def _xattn_kernel(q_ref, k_ref, v_ref, o_ref, *, n_heads):
    width = q_ref.shape[1]
    hd = width // n_heads
    scale = hd ** -0.5
    for h in range(n_heads):
        q = q_ref[:, h * hd:(h + 1) * hd]
        k = k_ref[:, h * hd:(h + 1) * hd]
        v = v_ref[:, h * hd:(h + 1) * hd]
        s = lax.dot_general(q, k, (((1,), (1,)), ((), ())), preferred_element_type=F32) * scale
        s = s - jnp.max(s, axis=-1, keepdims=True)
        p = jnp.exp(s)
        p = p / jnp.sum(p, axis=-1, keepdims=True)
        o = jnp.dot(p.astype(BF16), v, preferred_element_type=F32)
        o_ref[:, h * hd:(h + 1) * hd] = o.astype(o_ref.dtype)


def _memory_xattn(q_all, q_block, kv, *, batch, seq, n_mem, tq=512):
    t = q_all.shape[0]
    width = kv.shape[1] // 2
    tq = _tile(seq, tq)
    nq = seq // tq
    return pl.pallas_call(
        functools.partial(_xattn_kernel, n_heads=MEM_HEADS),
        grid=(batch, nq),
        in_specs=[pl.BlockSpec((tq, width), lambda b, i: (b * nq + i, q_block)),
                  pl.BlockSpec((n_mem, width), lambda b, i: (b, 0)),
                  pl.BlockSpec((n_mem, width), lambda b, i: (b, 1))],
        out_specs=pl.BlockSpec((tq, width), lambda b, i: (b * nq + i, 0)),
        out_shape=jax.ShapeDtypeStruct((t, width), BF16),
        compiler_params=_params("parallel", "parallel"),
        name="mem_xattn",
    )(q_all, kv, kv)


def _merge_kernel(ys_ref, yl_ref, ym_ref, ws_ref, wl_ref, wm_ref, g0_ref, g1_ref, g2_ref, bg_ref, o_ref):
    acc = None
    for k, (y_ref, w_ref, g_ref) in enumerate(((ys_ref, ws_ref, g0_ref), (yl_ref, wl_ref, g1_ref),
                                                (ym_ref, wm_ref, g2_ref))):
        gate = _sigmoid(g_ref[...].astype(F32) + bg_ref[k:k + 1, :])
        term = gate * jnp.dot(y_ref[...], w_ref[...], preferred_element_type=F32)
        acc = term if acc is None else acc + term
    o_ref[...] = acc.astype(o_ref.dtype)


def _merge(y_ssd, y_lru, y_mem, w_ssd, w_lru, w_mem, gate_pre, b_gate, *, tm=1024, tn=512):
    t, kdim = y_ssd.shape
    d = w_ssd.shape[1]
    tm, tn = _tile(t, tm), _tile(d, tn)
    nj = d // tn
    yspec = pl.BlockSpec((tm, kdim), lambda i, j: (i, 0))
    wspec = pl.BlockSpec((kdim, tn), lambda i, j: (0, j))
    gspec = lambda k: pl.BlockSpec((tm, tn), lambda i, j: (i, k * nj + j))
    return pl.pallas_call(
        _merge_kernel,
        grid=(t // tm, nj),
        in_specs=[yspec, yspec, yspec, wspec, wspec, wspec, gspec(0), gspec(1), gspec(2),
                  pl.BlockSpec((b_gate.shape[0], tn), lambda i, j: (0, j))],
        out_specs=pl.BlockSpec((tm, tn), lambda i, j: (i, j)),
        out_shape=jax.ShapeDtypeStruct((t, d), BF16),
        compiler_params=_params("parallel", "parallel"),
        name="branch_merge",
    )(y_ssd, y_lru, y_mem, w_ssd, w_lru, w_mem, gate_pre, gate_pre, gate_pre, b_gate)


def _layernorm_kernel(x_ref, g_ref, b_ref, o_ref, *rest):
    x = x_ref[...]
    mu = jnp.mean(x, axis=-1, keepdims=True)
    xc = x - mu
    var = jnp.mean(xc * xc, axis=-1, keepdims=True)
    y = xc * lax.rsqrt(var + LN_EPS) * g_ref[...] + b_ref[...]
    o_ref[...] = y
    if rest:
        rest[0][...] = y.astype(BF16)


def _layernorm(x, g, b, *, with_bf16, tm=256):
    t, d = x.shape
    tm = _tile(t, tm)
    spec = pl.BlockSpec((tm, d), lambda i: (i, 0))
    vspec = pl.BlockSpec((1, d), lambda i: (0, 0))
    out_shape = [jax.ShapeDtypeStruct((t, d), F32)]
    out_specs = [spec]
    if with_bf16:
        out_shape.append(jax.ShapeDtypeStruct((t, d), BF16))
        out_specs.append(spec)
    return pl.pallas_call(
        _layernorm_kernel,
        grid=(t // tm,),
        in_specs=[spec, vspec, vspec],
        out_specs=out_specs,
        out_shape=out_shape,
        compiler_params=_params("parallel"),
        name="layernorm",
    )(x, g.astype(F32).reshape(1, d), b.astype(F32).reshape(1, d))


def _extract_max(vals, ids):
    m = jnp.max(vals, axis=0, keepdims=True)
    sel = jnp.min(jnp.where(vals == m, ids, ID_SENTINEL), axis=0, keepdims=True)
    return m, sel, ids == sel


def _row_ids(shape, scale=1, offset=0):
    return ((lax.broadcasted_iota(I32, shape, 0) + offset) * scale).astype(F32)


def _peer_route_kernel(q_ref, keys_ref, hi_ref, lo_ref, gate_ref, ts_ref, ti_ref, bs_ref):
    tn = q_ref.shape[0]
    n_keys = keys_ref.shape[2]
    half = keys_ref.shape[3]
    k_top = PEER_TOPK
    key_id = _row_ids((n_keys, tn))

    for side in range(2):
        q = q_ref[:, side * half:(side + 1) * half]
        s = lax.dot_general(keys_ref[0, side], q, (((1,), (1,)), ((), ())), preferred_element_type=F32)
        for k in range(k_top):
            m, sel, hit = _extract_max(s, key_id)
            ts_ref[side, k:k + 1, :] = m
            ti_ref[side, k:k + 1, :] = sel
            s = jnp.where(hit, NEG_INF, s)

    s0, s1 = ts_ref[0], ts_ref[1]
    i0, i1 = ti_ref[0], ti_ref[1]
    half_k = k_top // 2
    cand = [s0[0:1] + s1]
    a_id = [jnp.broadcast_to(i0[0:1], (k_top, tn))]
    b_id = [i1]
    pos = [_row_ids((k_top, tn))]
    for i in range(1, half_k):
        cand.append(s0[i:i + 1] + s1[0:half_k])
        a_id.append(jnp.broadcast_to(i0[i:i + 1], (half_k, tn)))
        b_id.append(i1[0:half_k])
        pos.append(_row_ids((half_k, tn), offset=i * k_top))
    cand.append(s0[half_k:k_top] + s1[0:1])
    a_id.append(i0[half_k:k_top])
    b_id.append(jnp.broadcast_to(i1[0:1], (k_top - half_k, tn)))
    pos.append(_row_ids((k_top - half_k, tn), scale=k_top, offset=half_k))
    cand = jnp.concatenate(cand, axis=0)
    a_id = jnp.concatenate(a_id, axis=0)
    b_id = jnp.concatenate(b_id, axis=0)
    pos = jnp.concatenate(pos, axis=0)

    for k in range(k_top):
        m, _, hit = _extract_max(cand, pos)
        bs_ref[k:k + 1, :] = m
        hi_ref[k:k + 1, :] = jnp.sum(jnp.where(hit, a_id, 0.0), axis=0, keepdims=True).astype(I32)
        lo_ref[k:k + 1, :] = jnp.sum(jnp.where(hit, b_id, 0.0), axis=0, keepdims=True).astype(I32)
        cand = jnp.where(hit, NEG_INF, cand)

    bs = bs_ref[...]
    e = jnp.exp(bs - bs[0:1])
    gate_ref[...] = e / jnp.sum(e, axis=0, keepdims=True)


def _peer_route(q, keys, *, tn=512):
    t = q.shape[0]
    n_heads, _, n_keys, half = keys.shape
    assert n_keys == LANES and PEER_TOPK % SUBLANES == 0
    tn = _tile(t, tn)
    slots = n_heads * PEER_TOPK
    ospec = pl.BlockSpec((PEER_TOPK, tn), lambda i, h: (h, i))
    return pl.pallas_call(
        _peer_route_kernel,
        grid=(t // tn, n_heads),
        in_specs=[pl.BlockSpec((tn, 2 * half), lambda i, h: (i, h)),
                  pl.BlockSpec((1, 2, n_keys, half), lambda i, h: (h, 0, 0, 0))],
        out_specs=[ospec, ospec, ospec],
        out_shape=[jax.ShapeDtypeStruct((slots, t), I32), jax.ShapeDtypeStruct((slots, t), I32),
                   jax.ShapeDtypeStruct((slots, t), F32)],
        scratch_shapes=[pltpu.VMEM((2, PEER_TOPK, tn), F32), pltpu.VMEM((2, PEER_TOPK, tn), F32),
                        pltpu.VMEM((PEER_TOPK, tn), F32)],
        compiler_params=_params("parallel", "parallel"),
        name="peer_route",
    )(q, keys)


def _erf_gelu(v):
    return 0.5 * v * (1.0 + lax.erf(v * (2.0 ** -0.5)))


def _peer_act_kernel(h_ref, ut_ref, hi_ref, lo_ref, gate_ref, w_ref, acc_ref, *, n_chunks):
    j = pl.program_id(1)

    @pl.when(j == 0)
    def _():
        acc_ref[...] = jnp.zeros(acc_ref.shape, F32)

    dense = jnp.dot(h_ref[...], ut_ref[...], preferred_element_type=F32)
    hi = hi_ref[...]
    lo = lo_ref[...]
    acc = acc_ref[...]
    per_step = dense.shape[1] // LANES
    for c in range(per_step):
        picked = jnp.take_along_axis(dense[:, c * LANES:(c + 1) * LANES], lo, axis=1)
        acc = jnp.where(hi == j * per_step + c, picked, acc)
    acc_ref[...] = acc

    @pl.when(j == n_chunks - 1)
    def _():
        w_ref[...] = gate_ref[...] * _erf_gelu(acc_ref[...])


def _peer_act(h_bf, u_t, hi_t, lo_t, gate_t, *, tm=1024, te=1024):
    t, d = h_bf.shape
    n_exp = u_t.shape[1]
    slots = hi_t.shape[1]
    assert slots == LANES
    tm, te = _tile(t, tm), _tile(n_exp, te)
    n_chunks = n_exp // te
    sspec = pl.BlockSpec((tm, slots), lambda i, j: (i, 0))
    return pl.pallas_call(
        functools.partial(_peer_act_kernel, n_chunks=n_chunks),
        grid=(t // tm, n_chunks),
        in_specs=[pl.BlockSpec((tm, d), lambda i, j: (i, 0)),
                  pl.BlockSpec((d, te), lambda i, j: (0, j)),
                  sspec, sspec, sspec],
        out_specs=sspec,
        out_shape=jax.ShapeDtypeStruct((t, slots), F32),
        scratch_shapes=[pltpu.VMEM((tm, slots), F32)],
        compiler_params=_params("parallel", "arbitrary"),
        name="peer_act",
    )(h_bf, u_t, hi_t, lo_t, gate_t)


def _peer_scatter_kernel(hi_ref, lo_ref, w_ref, o_ref, w3_ref):
    tb, slots = hi_ref.shape
    n_keys = w3_ref.shape[1]
    row = lax.broadcasted_iota(I32, (tb, n_keys, slots), 1)
    hi = hi_ref[...][:, None, :]
    lo = lo_ref[...][:, None, :]
    wv = w_ref[...][:, None, :]
    a_t = jnp.where(row == hi, wv, 0.0).astype(BF16)
    r_t = jnp.where(row == lo, 1.0, 0.0).astype(BF16)
    w3_ref[...] = lax.dot_general(a_t, r_t, (((2,), (2,)), ((0,), (0,))), preferred_element_type=F32)
    for k in range(n_keys):
        o_ref[:, k * n_keys:(k + 1) * n_keys] = w3_ref[:, k, :].astype(o_ref.dtype)


def _peer_scatter(hi_t, lo_t, w_t, n_keys, *, tb=64):
    t, slots = hi_t.shape
    tb = _tile(t, tb, align=SUBLANES)
    sspec = pl.BlockSpec((tb, slots), lambda i: (i, 0))
    return pl.pallas_call(
        _peer_scatter_kernel,
        grid=(t // tb,),
        in_specs=[sspec, sspec, sspec],
        out_specs=pl.BlockSpec((tb, n_keys * n_keys), lambda i: (i, 0)),
        out_shape=jax.ShapeDtypeStruct((t, n_keys * n_keys), BF16),
        scratch_shapes=[pltpu.VMEM((tb, n_keys, n_keys), F32)],
        compiler_params=_params("parallel"),
        name="peer_scatter",
    )(hi_t, lo_t, w_t)


def _layer(h, mem, w_in, b_gate, ssd_conv_w, ssd_conv_b, ssd_dt_bias, ssd_a_log, ssd_d, ssd_norm_w,
           lru_conv_w, lru_conv_b, lru_w_a, lru_b_a, lru_w_i, lru_b_i, lru_lambda,
           mem_w_kv, w_branch_ssd, w_branch_lru, w_branch_mem, w_out, ln1_g, ln1_b,
           peer_w_q, peer_keys, peer_u, peer_v, ln2_g, ln2_b, *, depth):
    bsz, seq, d = h.shape
    t = bsz * seq
    n_mem = mem.shape[1]
    alpha = (2.0 * depth) ** 0.25

    d_inner = w_branch_ssd.shape[0]
    xbc_w = ssd_conv_w.shape[1]
    n_heads = ssd_dt_bias.shape[0]
    lru_w = w_branch_lru.shape[0]
    mem_w = w_branch_mem.shape[0]
    n_branch = b_gate.shape[0]
    c_dt = d_inner + xbc_w
    c_lru = c_dt + n_heads
    c_gate = c_lru + 2 * lru_w + mem_w
    assert w_in.shape[1] == c_gate + n_branch * d and n_branch == 3

    hf = h.reshape(t, d)
    h_bf = hf.astype(BF16)
    assert lru_w == mem_w
    w_xz = jnp.concatenate([w_in[:, d_inner:c_dt], w_in[:, :d_inner]], axis=1).astype(BF16)
    w_dt = jnp.pad(w_in[:, c_dt:c_lru], ((0, 0), (0, LANES - n_heads))).astype(BF16)
    w_lm = w_in[:, c_lru:c_gate].astype(BF16)
    w_g = w_in[:, c_gate:].astype(BF16)

    xz = _matmul(h_bf, w_xz, BF16, name="proj_xz")
    dt_pad = _matmul(h_bf, w_dt, F32, name="proj_dt")
    lm = _matmul(h_bf, w_lm, BF16, name="proj_lru_mem")
    gate_pre = _matmul(h_bf, w_g, BF16, name="proj_gates")

    y_ssd = _ssd_branch(xz, dt_pad, ssd_conv_w, ssd_conv_b, ssd_dt_bias,
                        ssd_a_log, ssd_d, ssd_norm_w, batch=bsz, seq=seq)
    y_lru = _rglru_branch(lm, lru_conv_w, lru_conv_b, lru_w_a, lru_b_a,
                          lru_w_i, lru_b_i, lru_lambda, batch=bsz, seq=seq)
    kv = _matmul(mem.reshape(bsz * n_mem, d).astype(BF16), mem_w_kv.astype(BF16), BF16, name="mem_kv")
    y_mem = _memory_xattn(lm, 2, kv, batch=bsz, seq=seq, n_mem=n_mem)

    merged = _merge(y_ssd, y_lru, y_mem, w_branch_ssd.astype(BF16), w_branch_lru.astype(BF16),
                    w_branch_mem.astype(BF16), gate_pre, b_gate.astype(F32))
    pre1 = _matmul(merged, w_out.astype(BF16), F32, tn=512, res=hf, res_scale=alpha, name="out_proj")
    h1, h1_bf = _layernorm(pre1, ln1_g, ln1_b, with_bf16=True)

    n_keys = peer_keys.shape[2]
    q = _matmul(h1_bf, peer_w_q.astype(BF16), BF16, name="peer_q")
    hi, lo, gate = _peer_route(q, peer_keys.astype(BF16))
    hi_t, lo_t, gate_t = hi.T, lo.T, gate.T
    w_t = _peer_act(h1_bf, peer_u.astype(BF16).T, hi_t, lo_t, gate_t)
    dense_w = _peer_scatter(hi_t, lo_t, w_t, n_keys)
    pre2 = _matmul(dense_w, peer_v.astype(BF16), F32, tn=512, tk=4096, res=h1, res_scale=alpha, name="peer_v")
    (out,) = _layernorm(pre2, ln2_g, ln2_b, with_bf16=False)
    return out.reshape(bsz, seq, d)


def kernel(x, mem, w_in, b_gate, ssd_conv_w, ssd_conv_b, ssd_dt_bias, ssd_a_log, ssd_d, ssd_norm_w, lru_conv_w, lru_conv_b, lru_w_a, lru_b_a, lru_w_i, lru_b_i, lru_lambda, mem_w_kv, w_branch_ssd, w_branch_lru, w_branch_mem, w_out, ln1_g, ln1_b, peer_w_q, peer_keys, peer_u, peer_v, ln2_g, ln2_b):
    params = (w_in, b_gate, ssd_conv_w, ssd_conv_b, ssd_dt_bias, ssd_a_log, ssd_d, ssd_norm_w, lru_conv_w,
              lru_conv_b, lru_w_a, lru_b_a, lru_w_i, lru_b_i, lru_lambda, mem_w_kv, w_branch_ssd, w_branch_lru,
              w_branch_mem, w_out, ln1_g, ln1_b, peer_w_q, peer_keys, peer_u, peer_v, ln2_g, ln2_b)
    depth = w_in.shape[0]
    h = x
    for l in range(depth):
        h = _layer(h, mem, *(p[l] for p in params), depth=depth)
    return h
```

```python
import functools
import math

import jax
import jax.numpy as jnp
from jax import lax
from jax.experimental import pallas as pl
from jax.experimental.pallas import tpu as pltpu

F32 = jnp.float32
BF16 = jnp.bfloat16
I32 = jnp.int32

SSD_HEAD_DIM = 64
SSD_D_STATE = 128
SSD_CONV = 4
LRU_CONV = 4
LRU_C = 8.0
MEM_HEADS = 4
PEER_TOPK = 16
LN_EPS = 1e-5
RMS_EPS = 1e-5

LANES = 128
SUBLANES = 8
VMEM_LIMIT_BYTES = 56 * 1024 * 1024

HIGHEST = lax.Precision.HIGHEST
NEG_INF = float("-inf")
ID_SENTINEL = 1e9


def _params(*semantics):
    return pltpu.CompilerParams(dimension_semantics=semantics, vmem_limit_bytes=VMEM_LIMIT_BYTES)


def _tile(dim, want, align=LANES):
    if dim <= want:
        return dim
    t = want - want % align
    while t > align and dim % t:
        t -= align
    assert dim % t == 0, (dim, want)
    return t


def _sigmoid(v):
    return 1.0 / (1.0 + jnp.exp(-v))


def _softplus(v):
    return jnp.maximum(v, 0.0) + jnp.log1p(jnp.exp(-jnp.abs(v)))


def _split3(v):
    hi = v.astype(BF16)
    rest = v - hi.astype(F32)
    mid = rest.astype(BF16)
    lo = (rest - mid.astype(F32)).astype(BF16)
    return hi, mid, lo


def _matmul_kernel(*refs, nk, res_scale, has_res):
    a_ref, b_ref = refs[0], refs[1]
    res_ref = refs[2] if has_res else None
    o_ref = refs[3] if has_res else refs[2]
    acc_ref = refs[-1] if nk > 1 else None

    def finish(acc):
        if has_res:
            acc = acc + res_scale * res_ref[...].astype(F32)
        o_ref[...] = acc.astype(o_ref.dtype)

    part = jnp.dot(a_ref[...], b_ref[...], preferred_element_type=F32)
    if nk == 1:
        finish(part)
        return
    k = pl.program_id(2)

    @pl.when(k == 0)
    def _():
        acc_ref[...] = part

    @pl.when(k > 0)
    def _():
        acc_ref[...] += part

    @pl.when(k == nk - 1)
    def _():
        finish(acc_ref[...])


def _matmul(a, b, out_dtype, *, tm=1024, tn=1024, tk=None, res=None, res_scale=1.0, name="matmul"):
    m, kdim = a.shape
    _, n = b.shape
    tm, tn = _tile(m, tm), _tile(n, tn)
    tk = kdim if tk is None else _tile(kdim, tk)
    nk = kdim // tk
    in_specs = [pl.BlockSpec((tm, tk), lambda i, j, k: (i, k)),
                pl.BlockSpec((tk, tn), lambda i, j, k: (k, j))]
    args = [a, b]
    if res is not None:
        in_specs.append(pl.BlockSpec((tm, tn), lambda i, j, k: (i, j)))
        args.append(res)
    return pl.pallas_call(
        functools.partial(_matmul_kernel, nk=nk, res_scale=res_scale, has_res=res is not None),
        grid=(m // tm, n // tn, nk),
        in_specs=in_specs,
        out_specs=pl.BlockSpec((tm, tn), lambda i, j, k: (i, j)),
        out_shape=jax.ShapeDtypeStruct((m, n), out_dtype),
        scratch_shapes=[pltpu.VMEM((tm, tn), F32)] if nk > 1 else [],
        compiler_params=_params("parallel", "parallel", "arbitrary"),
        name=name,
    )(*args)


def _causal_conv(x_ref, xpad_ref, w_ref, b_ref, first, n_taps):
    rows = x_ref.shape[0]
    x = x_ref[...].astype(F32)

    @pl.when(first)
    def _():
        xpad_ref[0:SUBLANES, :] = jnp.zeros((SUBLANES, x.shape[1]), F32)

    xpad_ref[SUBLANES:SUBLANES + rows, :] = x
    w = w_ref[...]
    y = b_ref[...] + w[n_taps - 1:n_taps, :] * x
    for back in range(1, n_taps):
        tap = n_taps - 1 - back
        y = y + w[tap:tap + 1, :] * xpad_ref[SUBLANES - back:SUBLANES - back + rows, :]
    xpad_ref[0:SUBLANES, :] = x[rows - SUBLANES:rows, :]
    return y


def _ssd_kernel(z_ref, xbc_ref, dt_ref, cw_ref, cb_ref, dtb_ref, alog_ref, dskip_ref, nw_ref, e_ref,
                y_ref, xpad_ref, state_ref, *, n_groups, heads_per_group):
    rows = z_ref.shape[0]
    d_inner = z_ref.shape[1]
    gn = n_groups * SSD_D_STATE
    gw = heads_per_group * SSD_HEAD_DIM
    first = pl.program_id(1) == 0

    @pl.when(first)
    def _():
        state_ref[...] = jnp.zeros(state_ref.shape, F32)

    conv = _causal_conv(xbc_ref, xpad_ref, cw_ref, cb_ref, first, SSD_CONV)
    act = conv * _sigmoid(conv)

    dtv = _softplus(dt_ref[...] + dtb_ref[...])
    da = dtv * (-jnp.exp(alog_ref[...]))
    r_i = lax.broadcasted_iota(I32, (rows, rows), 0)
    c_i = lax.broadcasted_iota(I32, (rows, rows), 1)
    causal = r_i >= c_i
    cs = jnp.dot(causal.astype(F32), da, precision=HIGHEST, preferred_element_type=F32)
    cs_end = cs[rows - 1:rows, :]
    cs_t = cs.T

    stack = jnp.concatenate([dtv, jnp.exp(cs_end - cs), jnp.exp(cs)], axis=0)
    ex = jnp.dot(jnp.concatenate(_split3(stack), axis=1), e_ref[...], preferred_element_type=F32)
    dt_x, dte_x, ecs_x = ex[0:rows], ex[rows:2 * rows], ex[2 * rows:3 * rows]

    xs = act[:, 0:d_inner]
    xdt = xs * dt_x
    xdt_end = xdt * dte_x
    lane = lax.broadcasted_iota(I32, (1, gw), 1)

    for g in range(n_groups):
        lo, hi = g * gw, (g + 1) * gw
        b_f = act[:, d_inner + g * SSD_D_STATE:d_inner + (g + 1) * SSD_D_STATE]
        c_g = act[:, d_inner + gn + g * SSD_D_STATE:d_inner + gn + (g + 1) * SSD_D_STATE].astype(BF16)
        b_g = b_f.astype(BF16)
        cb = lax.dot_general(c_g, b_g, (((1,), (1,)), ((), ())), preferred_element_type=F32)
        st = state_ref[g]
        y_off = jnp.dot(c_g, st.astype(BF16), preferred_element_type=F32) * ecs_x[:, lo:hi]
        xdt_g = xdt[:, lo:hi]
        y_diag = jnp.zeros((rows, gw), F32)
        for r in range(heads_per_group):
            h = g * heads_per_group + r
            seg = cs[:, h:h + 1] - cs_t[h:h + 1, :]
            decay = jnp.where(causal, jnp.exp(jnp.where(causal, seg, 0.0)), 0.0)
            m_h = (cb * decay).astype(BF16)
            in_head = (lane >= r * SSD_HEAD_DIM) & (lane < (r + 1) * SSD_HEAD_DIM)
            x_h = jnp.where(in_head, xdt_g, 0.0).astype(BF16)
            y_diag = y_diag + jnp.dot(m_h, x_h, preferred_element_type=F32)
        new_st = st * ecs_x[rows - 1:rows, lo:hi] + jnp.dot(
            b_f.T.astype(BF16), xdt_end[:, lo:hi].astype(BF16), preferred_element_type=F32)
        state_ref[g] = new_st

        y = y_diag + y_off + xs[:, lo:hi] * dskip_ref[:, lo:hi]
        zg = z_ref[:, lo:hi].astype(F32)
        y = y * (zg * _sigmoid(zg))
        ms = jnp.mean(y * y, axis=-1, keepdims=True)
        y = y * lax.rsqrt(ms + RMS_EPS) * nw_ref[:, lo:hi]
        y_ref[:, lo:hi] = y.astype(y_ref.dtype)


def _ssd_branch(xz, dt_pad, conv_w, conv_b, dt_bias, a_log, d_skip, norm_w, *, batch, seq, chunk=128):
    t = xz.shape[0]
    xbc_w = conv_w.shape[1]
    d_inner = xz.shape[1] - xbc_w
    assert xbc_w % d_inner == 0
    n_heads = dt_bias.shape[0]
    n_groups = (xbc_w - d_inner) // (2 * SSD_D_STATE)
    hpg = n_heads // n_groups
    assert d_inner == n_heads * SSD_HEAD_DIM and d_inner // n_groups == hpg * SSD_HEAD_DIM
    assert n_heads <= LANES
    chunk = _tile(seq, chunk)
    nc = seq // chunk
    pad = LANES - n_heads
    dtb = jnp.pad(dt_bias.astype(F32), (0, pad)).reshape(1, LANES)
    alog = jnp.pad(a_log.astype(F32), (0, pad)).reshape(1, LANES)
    dskip = jnp.repeat(d_skip.astype(F32), SSD_HEAD_DIM).reshape(1, d_inner)
    expand = (jnp.arange(LANES)[:, None] == (jnp.arange(d_inner)[None, :] // SSD_HEAD_DIM)).astype(BF16)
    expand = jnp.concatenate([expand] * 3, axis=0)
    row = lambda b, c: (b * nc + c, 0)
    fixed = lambda b, c: (0, 0)
    return pl.pallas_call(
        functools.partial(_ssd_kernel, n_groups=n_groups, heads_per_group=hpg),
        grid=(batch, nc),
        in_specs=[pl.BlockSpec((chunk, d_inner), lambda b, c: (b * nc + c, xbc_w // d_inner)),
                  pl.BlockSpec((chunk, xbc_w), row),
                  pl.BlockSpec((chunk, LANES), row),
                  pl.BlockSpec((SSD_CONV, xbc_w), fixed),
                  pl.BlockSpec((1, xbc_w), fixed),
                  pl.BlockSpec((1, LANES), fixed),
                  pl.BlockSpec((1, LANES), fixed),
                  pl.BlockSpec((1, d_inner), fixed),
                  pl.BlockSpec((1, d_inner), fixed),
                  pl.BlockSpec((3 * LANES, d_inner), fixed)],
        out_specs=pl.BlockSpec((chunk, d_inner), row),
        out_shape=jax.ShapeDtypeStruct((t, d_inner), BF16),
        scratch_shapes=[pltpu.VMEM((chunk + SUBLANES, xbc_w), F32),
                        pltpu.VMEM((n_groups, SSD_D_STATE, hpg * SSD_HEAD_DIM), F32)],
        compiler_params=_params("arbitrary", "arbitrary"),
        name="ssd_scan",
    )(xz, xz, dt_pad, conv_w.astype(F32), conv_b.astype(F32).reshape(1, xbc_w), dtb, alog, dskip,
      norm_w.astype(F32).reshape(1, d_inner), expand)


def _rglru_kernel(g_ref, x_ref, cw_ref, cb_ref, wa_ref, ba_ref, wi_ref, bi_ref, lam_ref,
                  y_ref, xpad_ref, carry_ref):
    rows, width = x_ref.shape
    n_blk = wa_ref.shape[0]
    blk = wa_ref.shape[1]
    first = pl.program_id(2) == 0

    @pl.when(first)
    def _():
        carry_ref[...] = jnp.zeros(carry_ref.shape, F32)

    xc = _causal_conv(x_ref, xpad_ref, cw_ref, cb_ref, first, LRU_CONV)
    xcb = xc.astype(BF16)
    ra = jnp.concatenate([jnp.dot(xcb[:, k * blk:(k + 1) * blk], wa_ref[k], preferred_element_type=F32)
                          for k in range(n_blk)], axis=1)
    ri = jnp.concatenate([jnp.dot(xcb[:, k * blk:(k + 1) * blk], wi_ref[k], preferred_element_type=F32)
                          for k in range(n_blk)], axis=1)
    r = _sigmoid(ra + ba_ref[...])
    i = _sigmoid(ri + bi_ref[...])
    log_a = (-LRU_C) * r * _softplus(-lam_ref[...])
    a = jnp.exp(log_a)
    q = -jnp.tanh(log_a) * (a * a + 1.0)
    u = jnp.where(q > 0.0, q * lax.rsqrt(q), 0.0) * (i * xc)

    row_in_tile = lax.broadcasted_iota(I32, (rows, width), 0) & (SUBLANES - 1)
    shift = 1
    while shift < SUBLANES:
        keep = row_in_tile >= shift
        a_prev = jnp.where(keep, pltpu.roll(a, shift, 0), 1.0)
        u_prev = jnp.where(keep, pltpu.roll(u, shift, 0), 0.0)
        u = a * u_prev + u
        a = a * a_prev
        shift *= 2
    carry = carry_ref[0:1, :]
    tiles = []
    for g in range(rows // SUBLANES):
        rs = slice(g * SUBLANES, (g + 1) * SUBLANES)
        h_g = u[rs] + a[rs] * carry
        tiles.append(h_g)
        carry = h_g[SUBLANES - 1:SUBLANES, :]
    h = jnp.concatenate(tiles, axis=0)
    carry_ref[...] = jnp.broadcast_to(carry, carry_ref.shape)

    gv = g_ref[...].astype(F32)
    gelu = 0.5 * gv * (1.0 + jnp.tanh(math.sqrt(2.0 / math.pi) * (gv + 0.044715 * (gv * gv * gv))))
    y_ref[...] = (h * gelu).astype(y_ref.dtype)


def _rglru_branch(gx, conv_w, conv_b, w_a, b_a, w_i, b_i, lam, *, batch, seq, chunk=256, cw=512):
    t = gx.shape[0]
    width = conv_w.shape[1]
    n_blocks, blk, _ = w_a.shape
    cw = _tile(width, cw)
    assert cw % blk == 0
    bpc = cw // blk
    chunk = _tile(seq, chunk)
    nc = seq // chunk
    row = lambda b, j, c: (b * nc + c, j)
    col = lambda b, j, c: (0, j)
    wblk = lambda b, j, c: (j, 0, 0)
    vec = lambda v: v.astype(F32).reshape(1, width)
    return pl.pallas_call(
        _rglru_kernel,
        grid=(batch, width // cw, nc),
        in_specs=[pl.BlockSpec((chunk, cw), row),
                  pl.BlockSpec((chunk, cw), lambda b, j, c: (b * nc + c, width // cw + j)),
                  pl.BlockSpec((LRU_CONV, cw), col),
                  pl.BlockSpec((1, cw), col),
                  pl.BlockSpec((bpc, blk, blk), wblk),
                  pl.BlockSpec((1, cw), col),
                  pl.BlockSpec((bpc, blk, blk), wblk),
                  pl.BlockSpec((1, cw), col),
                  pl.BlockSpec((1, cw), col)],
        out_specs=pl.BlockSpec((chunk, cw), row),
        out_shape=jax.ShapeDtypeStruct((t, width), BF16),
        scratch_shapes=[pltpu.VMEM((chunk + SUBLANES, cw), F32), pltpu.VMEM((SUBLANES, cw), F32)],
        compiler_params=_params("arbitrary", "arbitrary", "arbitrary"),
        name="rglru_scan",
    )(gx, gx, conv_w.astype(F32), vec(conv_b), w_a.astype(BF16), vec(b_a), w_i.astype(BF16), vec(b_i),
      vec(lam))


def _xattn_kernel(q_ref, k_ref, v_ref, o_ref, *, n_heads):
    width = q_ref.shape[1]
    hd = width // n_heads
    scale = hd ** -0.5
    for h in range(n_heads):
        q = q_ref[:, h * hd:(h + 1) * hd]
        k = k_ref[:, h * hd:(h + 1) * hd]
        v = v_ref[:, h * hd:(h + 1) * hd]
        s = lax.dot_general(q, k, (((1,), (1,)), ((), ())), preferred_element_type=F32) * scale
        s = s - jnp.max(s, axis=-1, keepdims=True)
        p = jnp.exp(s)
        p = p / jnp.sum(p, axis=-1, keepdims=True)
        o = jnp.dot(p.astype(BF16), v, preferred_element_type=F32)
        o_ref[:, h * hd:(h + 1) * hd] = o.astype(o_ref.dtype)


def _memory_xattn(q_all, q_block, kv, *, batch, seq, n_mem, tq=512):
    t = q_all.shape[0]
    width = kv.shape[1] // 2
    tq = _tile(seq, tq)
    nq = seq // tq
    return pl.pallas_call(
        functools.partial(_xattn_kernel, n_heads=MEM_HEADS),
        grid=(batch, nq),
        in_specs=[pl.BlockSpec((tq, width), lambda b, i: (b * nq + i, q_block)),
                  pl.BlockSpec((n_mem, width), lambda b, i: (b, 0)),
                  pl.BlockSpec((n_mem, width), lambda b, i: (b, 1))],
        out_specs=pl.BlockSpec((tq, width), lambda b, i: (b * nq + i, 0)),
        out_shape=jax.ShapeDtypeStruct((t, width), BF16),
        compiler_params=_params("parallel", "parallel"),
        name="mem_xattn",
    )(q_all, kv, kv)


def _merge_kernel(ys_ref, yl_ref, ym_ref, ws_ref, wl_ref, wm_ref, g0_ref, g1_ref, g2_ref, bg_ref, o_ref):
    acc = None
    for k, (y_ref, w_ref, g_ref) in enumerate(((ys_ref, ws_ref, g0_ref), (yl_ref, wl_ref, g1_ref),
                                                (ym_ref, wm_ref, g2_ref))):
        gate = _sigmoid(g_ref[...].astype(F32) + bg_ref[k:k + 1, :])
        term = gate * jnp.dot(y_ref[...], w_ref[...], preferred_element_type=F32)
        acc = term if acc is None else acc + term
    o_ref[...] = acc.astype(o_ref.dtype)


def _merge(y_ssd, y_lru, y_mem, w_ssd, w_lru, w_mem, gate_pre, b_gate, *, tm=1024, tn=512):
    t, kdim = y_ssd.shape
    d = w_ssd.shape[1]
    tm, tn = _tile(t, tm), _tile(d, tn)
    nj = d // tn
    yspec = pl.BlockSpec((tm, kdim), lambda i, j: (i, 0))
    wspec = pl.BlockSpec((kdim, tn), lambda i, j: (0, j))
    gspec = lambda k: pl.BlockSpec((tm, tn), lambda i, j: (i, k * nj + j))
    return pl.pallas_call(
        _merge_kernel,
        grid=(t // tm, nj),
        in_specs=[yspec, yspec, yspec, wspec, wspec, wspec, gspec(0), gspec(1), gspec(2),
                  pl.BlockSpec((b_gate.shape[0], tn), lambda i, j: (0, j))],
        out_specs=pl.BlockSpec((tm, tn), lambda i, j: (i, j)),
        out_shape=jax.ShapeDtypeStruct((t, d), BF16),
        compiler_params=_params("parallel", "parallel"),
        name="branch_merge",
    )(y_ssd, y_lru, y_mem, w_ssd, w_lru, w_mem, gate_pre, gate_pre, gate_pre, b_gate)


def _layernorm_kernel(x_ref, g_ref, b_ref, o_ref, *rest):
    x = x_ref[...]
    mu = jnp.mean(x, axis=-1, keepdims=True)
    xc = x - mu
    var = jnp.mean(xc * xc, axis=-1, keepdims=True)
    y = xc * lax.rsqrt(var + LN_EPS) * g_ref[...] + b_ref[...]
    o_ref[...] = y
    if rest:
        rest[0][...] = y.astype(BF16)


def _layernorm(x, g, b, *, with_bf16, tm=256):
    t, d = x.shape
    tm = _tile(t, tm)
    spec = pl.BlockSpec((tm, d), lambda i: (i, 0))
    vspec = pl.BlockSpec((1, d), lambda i: (0, 0))
    out_shape = [jax.ShapeDtypeStruct((t, d), F32)]
    out_specs = [spec]
    if with_bf16:
        out_shape.append(jax.ShapeDtypeStruct((t, d), BF16))
        out_specs.append(spec)
    return pl.pallas_call(
        _layernorm_kernel,
        grid=(t // tm,),
        in_specs=[spec, vspec, vspec],
        out_specs=out_specs,
        out_shape=out_shape,
        compiler_params=_params("parallel"),
        name="layernorm",
    )(x, g.astype(F32).reshape(1, d), b.astype(F32).reshape(1, d))


def _extract_max(vals, ids):
    m = jnp.max(vals, axis=0, keepdims=True)
    sel = jnp.min(jnp.where(vals == m, ids, ID_SENTINEL), axis=0, keepdims=True)
    return m, sel, ids == sel


def _row_ids(shape, scale=1, offset=0):
    return ((lax.broadcasted_iota(I32, shape, 0) + offset) * scale).astype(F32)


def _peer_route_kernel(q_ref, keys_ref, hi_ref, lo_ref, gate_ref, ts_ref, ti_ref, bs_ref):
    tn = q_ref.shape[0]
    n_keys = keys_ref.shape[2]
    half = keys_ref.shape[3]
    k_top = PEER_TOPK
    key_id = _row_ids((n_keys, tn))

    for side in range(2):
        q = q_ref[:, side * half:(side + 1) * half]
        s = lax.dot_general(keys_ref[0, side], q, (((1,), (1,)), ((), ())), preferred_element_type=F32)
        for k in range(k_top):
            m, sel, hit = _extract_max(s, key_id)
            ts_ref[side, k:k + 1, :] = m
            ti_ref[side, k:k + 1, :] = sel
            s = jnp.where(hit, NEG_INF, s)

    s0, s1 = ts_ref[0], ts_ref[1]
    i0, i1 = ti_ref[0], ti_ref[1]
    half_k = k_top // 2
    cand = [s0[0:1] + s1]
    a_id = [jnp.broadcast_to(i0[0:1], (k_top, tn))]
    b_id = [i1]
    pos = [_row_ids((k_top, tn))]
    for i in range(1, half_k):
        cand.append(s0[i:i + 1] + s1[0:half_k])
        a_id.append(jnp.broadcast_to(i0[i:i + 1], (half_k, tn)))
        b_id.append(i1[0:half_k])
        pos.append(_row_ids((half_k, tn), offset=i * k_top))
    cand.append(s0[half_k:k_top] + s1[0:1])
    a_id.append(i0[half_k:k_top])
    b_id.append(jnp.broadcast_to(i1[0:1], (k_top - half_k, tn)))
    pos.append(_row_ids((k_top - half_k, tn), scale=k_top, offset=half_k))
    cand = jnp.concatenate(cand, axis=0)
    a_id = jnp.concatenate(a_id, axis=0)
    b_id = jnp.concatenate(b_id, axis=0)
    pos = jnp.concatenate(pos, axis=0)

    for k in range(k_top):
        m, _, hit = _extract_max(cand, pos)
        bs_ref[k:k + 1, :] = m
        hi_ref[k:k + 1, :] = jnp.sum(jnp.where(hit, a_id, 0.0), axis=0, keepdims=True).astype(I32)
        lo_ref[k:k + 1, :] = jnp.sum(jnp.where(hit, b_id, 0.0), axis=0, keepdims=True).astype(I32)
        cand = jnp.where(hit, NEG_INF, cand)

    bs = bs_ref[...]
    e = jnp.exp(bs - bs[0:1])
    gate_ref[...] = e / jnp.sum(e, axis=0, keepdims=True)


def _peer_route(q, keys, *, tn=1024):
    t = q.shape[0]
    n_heads, _, n_keys, half = keys.shape
    assert n_keys == LANES and PEER_TOPK % SUBLANES == 0
    tn = _tile(t, tn)
    slots = n_heads * PEER_TOPK
    ospec = pl.BlockSpec((PEER_TOPK, tn), lambda i, h: (h, i))
    return pl.pallas_call(
        _peer_route_kernel,
        grid=(t // tn, n_heads),
        in_specs=[pl.BlockSpec((tn, 2 * half), lambda i, h: (i, h)),
                  pl.BlockSpec((1, 2, n_keys, half), lambda i, h: (h, 0, 0, 0))],
        out_specs=[ospec, ospec, ospec],
        out_shape=[jax.ShapeDtypeStruct((slots, t), I32), jax.ShapeDtypeStruct((slots, t), I32),
                   jax.ShapeDtypeStruct((slots, t), F32)],
        scratch_shapes=[pltpu.VMEM((2, PEER_TOPK, tn), F32), pltpu.VMEM((2, PEER_TOPK, tn), F32),
                        pltpu.VMEM((PEER_TOPK, tn), F32)],
        compiler_params=_params("parallel", "parallel"),
        name="peer_route",
    )(q, keys)


def _erf_gelu(v):
    return 0.5 * v * (1.0 + lax.erf(v * (2.0 ** -0.5)))


def _peer_act_kernel(h_ref, u_ref, hi_ref, lo_ref, gate_ref, w_ref, acc_ref, *, n_chunks):
    j = pl.program_id(1)

    @pl.when(j == 0)
    def _():
        acc_ref[...] = jnp.zeros(acc_ref.shape, F32)

    dense = lax.dot_general(h_ref[...], u_ref[...], (((1,), (1,)), ((), ())),
                            preferred_element_type=F32)
    hi = hi_ref[...]
    lo = lo_ref[...]
    acc = acc_ref[...]
    per_step = dense.shape[1] // LANES
    for c in range(per_step):
        picked = jnp.take_along_axis(dense[:, c * LANES:(c + 1) * LANES], lo, axis=1)
        acc = jnp.where(hi == j * per_step + c, picked, acc)
    acc_ref[...] = acc

    @pl.when(j == n_chunks - 1)
    def _():
        w_ref[...] = gate_ref[...] * _erf_gelu(acc_ref[...])


def _peer_act(h_bf, u_tab, hi_t, lo_t, gate_t, *, tm=1024, te=1024):
    t, d = h_bf.shape
    n_exp = u_tab.shape[0]
    slots = hi_t.shape[1]
    assert slots == LANES
    tm, te = _tile(t, tm), _tile(n_exp, te)
    n_chunks = n_exp // te
    sspec = pl.BlockSpec((tm, slots), lambda i, j: (i, 0))
    return pl.pallas_call(
        functools.partial(_peer_act_kernel, n_chunks=n_chunks),
        grid=(t // tm, n_chunks),
        in_specs=[pl.BlockSpec((tm, d), lambda i, j: (i, 0)),
                  pl.BlockSpec((te, d), lambda i, j: (j, 0)),
                  sspec, sspec, sspec],
        out_specs=sspec,
        out_shape=jax.ShapeDtypeStruct((t, slots), F32),
        scratch_shapes=[pltpu.VMEM((tm, slots), F32)],
        compiler_params=_params("parallel", "arbitrary"),
        name="peer_act",
    )(h_bf, u_tab, hi_t, lo_t, gate_t)


def _peer_scatter_kernel(hi_ref, lo_ref, w_ref, o_ref, w3_ref, *, n_keys, pitch):
    tb, slots = hi_ref.shape
    row = lax.broadcasted_iota(I32, (tb, n_keys, slots), 1)
    hi = hi_ref[...][:, None, :]
    lo = lo_ref[...][:, None, :]
    wv = w_ref[...][:, None, :]
    a_t = jnp.where(row == hi, wv, 0.0).astype(BF16)
    r_t = jnp.where(row == lo, 1.0, 0.0).astype(BF16)
    w3 = lax.dot_general(a_t, r_t, (((2,), (2,)), ((0,), (0,))), preferred_element_type=F32)
    for t in range(tb):
        w3_ref[t * pitch:t * pitch + n_keys, :] = w3[t]
    for k in range(n_keys):
        o_ref[:, k * n_keys:(k + 1) * n_keys] = w3_ref[pl.ds(k, tb, stride=pitch), :].astype(o_ref.dtype)


def _peer_scatter(hi_t, lo_t, w_t, n_keys, *, tb=64):
    t, slots = hi_t.shape
    assert n_keys == LANES
    tb = _tile(t, tb, align=SUBLANES)
    pitch = n_keys + SUBLANES
    sspec = pl.BlockSpec((tb, slots), lambda i: (i, 0))
    return pl.pallas_call(
        functools.partial(_peer_scatter_kernel, n_keys=n_keys, pitch=pitch),
        grid=(t // tb,),
        in_specs=[sspec, sspec, sspec],
        out_specs=pl.BlockSpec((tb, n_keys * n_keys), lambda i: (i, 0)),
        out_shape=jax.ShapeDtypeStruct((t, n_keys * n_keys), BF16),
        scratch_shapes=[pltpu.VMEM((tb * pitch, n_keys), F32)],
        compiler_params=_params("parallel"),
        name="peer_scatter",
    )(hi_t, lo_t, w_t)


def _layer(h, mem, w_in, b_gate, ssd_conv_w, ssd_conv_b, ssd_dt_bias, ssd_a_log, ssd_d, ssd_norm_w,
           lru_conv_w, lru_conv_b, lru_w_a, lru_b_a, lru_w_i, lru_b_i, lru_lambda,
           mem_w_kv, w_branch_ssd, w_branch_lru, w_branch_mem, w_out, ln1_g, ln1_b,
           peer_w_q, peer_keys, peer_u, peer_v, ln2_g, ln2_b, *, depth):
    bsz, seq, d = h.shape
    t = bsz * seq
    n_mem = mem.shape[1]
    alpha = (2.0 * depth) ** 0.25

    d_inner = w_branch_ssd.shape[0]
    xbc_w = ssd_conv_w.shape[1]
    n_heads = ssd_dt_bias.shape[0]
    lru_w = w_branch_lru.shape[0]
    mem_w = w_branch_mem.shape[0]
    n_branch = b_gate.shape[0]
    c_dt = d_inner + xbc_w
    c_lru = c_dt + n_heads
    c_gate = c_lru + 2 * lru_w + mem_w
    assert w_in.shape[1] == c_gate + n_branch * d and n_branch == 3

    hf = h.reshape(t, d)
    h_bf = hf.astype(BF16)
    assert lru_w == mem_w
    w_xz = jnp.concatenate([w_in[:, d_inner:c_dt], w_in[:, :d_inner]], axis=1).astype(BF16)
    w_dt = jnp.pad(w_in[:, c_dt:c_lru], ((0, 0), (0, LANES - n_heads))).astype(BF16)
    w_lm = w_in[:, c_lru:c_gate].astype(BF16)
    w_g = w_in[:, c_gate:].astype(BF16)

    xz = _matmul(h_bf, w_xz, BF16, name="proj_xz")
    dt_pad = _matmul(h_bf, w_dt, F32, name="proj_dt")
    lm = _matmul(h_bf, w_lm, BF16, name="proj_lru_mem")
    gate_pre = _matmul(h_bf, w_g, BF16, name="proj_gates")

    y_ssd = _ssd_branch(xz, dt_pad, ssd_conv_w, ssd_conv_b, ssd_dt_bias,
                        ssd_a_log, ssd_d, ssd_norm_w, batch=bsz, seq=seq)
    y_lru = _rglru_branch(lm, lru_conv_w, lru_conv_b, lru_w_a, lru_b_a,
                          lru_w_i, lru_b_i, lru_lambda, batch=bsz, seq=seq)
    kv = _matmul(mem.reshape(bsz * n_mem, d).astype(BF16), mem_w_kv.astype(BF16), BF16, name="mem_kv")
    y_mem = _memory_xattn(lm, 2, kv, batch=bsz, seq=seq, n_mem=n_mem)

    merged = _merge(y_ssd, y_lru, y_mem, w_branch_ssd.astype(BF16), w_branch_lru.astype(BF16),
                    w_branch_mem.astype(BF16), gate_pre, b_gate.astype(F32))
    pre1 = _matmul(merged, w_out.astype(BF16), F32, tn=512, res=hf, res_scale=alpha, name="out_proj")
    h1, h1_bf = _layernorm(pre1, ln1_g, ln1_b, with_bf16=True)

    n_keys = peer_keys.shape[2]
    q = _matmul(h1_bf, peer_w_q.astype(BF16), BF16, name="peer_q")
    hi, lo, gate = _peer_route(q, peer_keys.astype(BF16))
    hi_t, lo_t, gate_t = hi.T, lo.T, gate.T
    w_t = _peer_act(h1_bf, peer_u.astype(BF16), hi_t, lo_t, gate_t)
    dense_w = _peer_scatter(hi_t, lo_t, w_t, n_keys)
    pre2 = _matmul(dense_w, peer_v.astype(BF16), F32, tk=2048, res=h1, res_scale=alpha, name="peer_v")
    (out,) = _layernorm(pre2, ln2_g, ln2_b, with_bf16=False)
    return out.reshape(bsz, seq, d)


def kernel(x, mem, w_in, b_gate, ssd_conv_w, ssd_conv_b, ssd_dt_bias, ssd_a_log, ssd_d, ssd_norm_w, lru_conv_w, lru_conv_b, lru_w_a, lru_b_a, lru_w_i, lru_b_i, lru_lambda, mem_w_kv, w_branch_ssd, w_branch_lru, w_branch_mem, w_out, ln1_g, ln1_b, peer_w_q, peer_keys, peer_u, peer_v, ln2_g, ln2_b):
    params = (w_in, b_gate, ssd_conv_w, ssd_conv_b, ssd_dt_bias, ssd_a_log, ssd_d, ssd_norm_w, lru_conv_w,
              lru_conv_b, lru_w_a, lru_b_a, lru_w_i, lru_b_i, lru_lambda, mem_w_kv, w_branch_ssd, w_branch_lru,
              w_branch_mem, w_out, ln1_g, ln1_b, peer_w_q, peer_keys, peer_u, peer_v, ln2_g, ln2_b)
    depth = w_in.shape[0]
    h = x
    for l in range(depth):
        h = _layer(h, mem, *(p[l] for p in params), depth=depth)
    return h
```

```python
import functools
import math

import jax
import jax.numpy as jnp
from jax import lax
from jax.experimental import pallas as pl
from jax.experimental.pallas import tpu as pltpu

F32 = jnp.float32
BF16 = jnp.bfloat16
I32 = jnp.int32

SSD_HEAD_DIM = 64
SSD_D_STATE = 128
SSD_CONV = 4
LRU_CONV = 4
LRU_C = 8.0
MEM_HEADS = 4
PEER_TOPK = 16
LN_EPS = 1e-5
RMS_EPS = 1e-5

LANES = 128
SUBLANES = 8
VMEM_LIMIT_BYTES = 56 * 1024 * 1024

HIGHEST = lax.Precision.HIGHEST
NEG_INF = float("-inf")
ID_SENTINEL = 1e9


def _params(*semantics):
    return pltpu.CompilerParams(dimension_semantics=semantics, vmem_limit_bytes=VMEM_LIMIT_BYTES)


def _tile(dim, want, align=LANES):
    if dim <= want:
        return dim
    t = want - want % align
    while t > align and dim % t:
        t -= align
    assert dim % t == 0, (dim, want)
    return t


def _sigmoid(v):
    return 1.0 / (1.0 + jnp.exp(-v))


def _softplus(v):
    return jnp.maximum(v, 0.0) + jnp.log1p(jnp.exp(-jnp.abs(v)))


def _split3(v):
    hi = v.astype(BF16)
    rest = v - hi.astype(F32)
    mid = rest.astype(BF16)
    lo = (rest - mid.astype(F32)).astype(BF16)
    return hi, mid, lo


def _matmul_kernel(a_ref, b_ref, *rest, res_scale, b_is_transposed):
    o_ref = rest[-1]
    contract_b = 1 if b_is_transposed else 0
    acc = lax.dot_general(a_ref[...], b_ref[...], (((1,), (contract_b,)), ((), ())), preferred_element_type=F32)
    if len(rest) == 2:
        acc = acc + res_scale * rest[0][...].astype(F32)
    o_ref[...] = acc.astype(o_ref.dtype)


def _matmul(a, b, out_dtype, *, tm=1024, tn=1024, b_is_transposed=False, res=None, res_scale=1.0, name="matmul"):
    m, kdim = a.shape
    n = b.shape[0] if b_is_transposed else b.shape[1]
    tm, tn = _tile(m, tm), _tile(n, tn)
    b_spec = (pl.BlockSpec((tn, kdim), lambda i, j: (j, 0)) if b_is_transposed
              else pl.BlockSpec((kdim, tn), lambda i, j: (0, j)))
    in_specs = [pl.BlockSpec((tm, kdim), lambda i, j: (i, 0)), b_spec]
    args = [a, b]
    if res is not None:
        in_specs.append(pl.BlockSpec((tm, tn), lambda i, j: (i, j)))
        args.append(res)
    return pl.pallas_call(
        functools.partial(_matmul_kernel, res_scale=res_scale, b_is_transposed=b_is_transposed),
        grid=(m // tm, n // tn),
        in_specs=in_specs,
        out_specs=pl.BlockSpec((tm, tn), lambda i, j: (i, j)),
        out_shape=jax.ShapeDtypeStruct((m, n), out_dtype),
        compiler_params=_params("parallel", "parallel"),
        name=name,
    )(*args)


def _causal_conv(x_ref, xpad_ref, w_ref, b_ref, first, n_taps):
    rows = x_ref.shape[0]
    x = x_ref[...].astype(F32)

    @pl.when(first)
    def _():
        xpad_ref[0:SUBLANES, :] = jnp.zeros((SUBLANES, x.shape[1]), F32)

    xpad_ref[SUBLANES:SUBLANES + rows, :] = x
    w = w_ref[...]
    y = b_ref[...] + w[n_taps - 1:n_taps, :] * x
    for back in range(1, n_taps):
        tap = n_taps - 1 - back
        y = y + w[tap:tap + 1, :] * xpad_ref[SUBLANES - back:SUBLANES - back + rows, :]
    xpad_ref[0:SUBLANES, :] = x[rows - SUBLANES:rows, :]
    return y


def _ssd_kernel(z_ref, xbc_ref, dt_ref, cw_ref, cb_ref, dtb_ref, alog_ref, dskip_ref, nw_ref, e_ref,
                y_ref, xpad_ref, state_ref, *, n_groups, heads_per_group):
    rows = z_ref.shape[0]
    d_inner = z_ref.shape[1]
    gn = n_groups * SSD_D_STATE
    gw = heads_per_group * SSD_HEAD_DIM
    first = pl.program_id(1) == 0

    @pl.when(first)
    def _():
        state_ref[...] = jnp.zeros(state_ref.shape, F32)

    conv = _causal_conv(xbc_ref, xpad_ref, cw_ref, cb_ref, first, SSD_CONV)
    act = conv * _sigmoid(conv)

    dtv = _softplus(dt_ref[...] + dtb_ref[...])
    da = dtv * (-jnp.exp(alog_ref[...]))
    r_i = lax.broadcasted_iota(I32, (rows, rows), 0)
    c_i = lax.broadcasted_iota(I32, (rows, rows), 1)
    causal = r_i >= c_i
    cs = jnp.dot(causal.astype(F32), da, precision=HIGHEST, preferred_element_type=F32)
    cs_end = cs[rows - 1:rows, :]
    cs_t = cs.T

    stack = jnp.concatenate([dtv, jnp.exp(cs_end - cs), jnp.exp(cs)], axis=0)
    ex = jnp.dot(jnp.concatenate(_split3(stack), axis=1), e_ref[...], preferred_element_type=F32)
    dt_x, dte_x, ecs_x = ex[0:rows], ex[rows:2 * rows], ex[2 * rows:3 * rows]

    xs = act[:, 0:d_inner]
    xdt = xs * dt_x
    xdt_end = xdt * dte_x
    lane = lax.broadcasted_iota(I32, (1, gw), 1)

    for g in range(n_groups):
        lo, hi = g * gw, (g + 1) * gw
        b_f = act[:, d_inner + g * SSD_D_STATE:d_inner + (g + 1) * SSD_D_STATE]
        c_g = act[:, d_inner + gn + g * SSD_D_STATE:d_inner + gn + (g + 1) * SSD_D_STATE].astype(BF16)
        b_g = b_f.astype(BF16)
        cb = lax.dot_general(c_g, b_g, (((1,), (1,)), ((), ())), preferred_element_type=F32)
        st = state_ref[g]
        y_off = jnp.dot(c_g, st.astype(BF16), preferred_element_type=F32) * ecs_x[:, lo:hi]
        xdt_g = xdt[:, lo:hi]
        y_diag = jnp.zeros((rows, gw), F32)
        for r in range(heads_per_group):
            h = g * heads_per_group + r
            seg = cs[:, h:h + 1] - cs_t[h:h + 1, :]
            decay = jnp.where(causal, jnp.exp(jnp.where(causal, seg, 0.0)), 0.0)
            m_h = (cb * decay).astype(BF16)
            in_head = (lane >= r * SSD_HEAD_DIM) & (lane < (r + 1) * SSD_HEAD_DIM)
            x_h = jnp.where(in_head, xdt_g, 0.0).astype(BF16)
            y_diag = y_diag + jnp.dot(m_h, x_h, preferred_element_type=F32)
        new_st = st * ecs_x[rows - 1:rows, lo:hi] + jnp.dot(
            b_f.T.astype(BF16), xdt_end[:, lo:hi].astype(BF16), preferred_element_type=F32)
        state_ref[g] = new_st

        y = y_diag + y_off + xs[:, lo:hi] * dskip_ref[:, lo:hi]
        zg = z_ref[:, lo:hi].astype(F32)
        y = y * (zg * _sigmoid(zg))
        ms = jnp.mean(y * y, axis=-1, keepdims=True)
        y = y * lax.rsqrt(ms + RMS_EPS) * nw_ref[:, lo:hi]
        y_ref[:, lo:hi] = y.astype(y_ref.dtype)


def _ssd_branch(xz, dt_pad, conv_w, conv_b, dt_bias, a_log, d_skip, norm_w, *, batch, seq, chunk=128):
    t = xz.shape[0]
    xbc_w = conv_w.shape[1]
    d_inner = xz.shape[1] - xbc_w
    assert xbc_w % d_inner == 0
    n_heads = dt_bias.shape[0]
    n_groups = (xbc_w - d_inner) // (2 * SSD_D_STATE)
    hpg = n_heads // n_groups
    assert d_inner == n_heads * SSD_HEAD_DIM and d_inner // n_groups == hpg * SSD_HEAD_DIM
    assert n_heads <= LANES
    chunk = _tile(seq, chunk)
    nc = seq // chunk
    pad = LANES - n_heads
    dtb = jnp.pad(dt_bias.astype(F32), (0, pad)).reshape(1, LANES)
    alog = jnp.pad(a_log.astype(F32), (0, pad)).reshape(1, LANES)
    dskip = jnp.repeat(d_skip.astype(F32), SSD_HEAD_DIM).reshape(1, d_inner)
    expand = (jnp.arange(LANES)[:, None] == (jnp.arange(d_inner)[None, :] // SSD_HEAD_DIM)).astype(BF16)
    expand = jnp.concatenate([expand] * 3, axis=0)
    row = lambda b, c: (b * nc + c, 0)
    fixed = lambda b, c: (0, 0)
    return pl.pallas_call(
        functools.partial(_ssd_kernel, n_groups=n_groups, heads_per_group=hpg),
        grid=(batch, nc),
        in_specs=[pl.BlockSpec((chunk, d_inner), lambda b, c: (b * nc + c, xbc_w // d_inner)),
                  pl.BlockSpec((chunk, xbc_w), row),
                  pl.BlockSpec((chunk, LANES), row),
                  pl.BlockSpec((SSD_CONV, xbc_w), fixed),
                  pl.BlockSpec((1, xbc_w), fixed),
                  pl.BlockSpec((1, LANES), fixed),
                  pl.BlockSpec((1, LANES), fixed),
                  pl.BlockSpec((1, d_inner), fixed),
                  pl.BlockSpec((1, d_inner), fixed),
                  pl.BlockSpec((3 * LANES, d_inner), fixed)],
        out_specs=pl.BlockSpec((chunk, d_inner), row),
        out_shape=jax.ShapeDtypeStruct((t, d_inner), BF16),
        scratch_shapes=[pltpu.VMEM((chunk + SUBLANES, xbc_w), F32),
                        pltpu.VMEM((n_groups, SSD_D_STATE, hpg * SSD_HEAD_DIM), F32)],
        compiler_params=_params("arbitrary", "arbitrary"),
        name="ssd_scan",
    )(xz, xz, dt_pad, conv_w.astype(F32), conv_b.astype(F32).reshape(1, xbc_w), dtb, alog, dskip,
      norm_w.astype(F32).reshape(1, d_inner), expand)


def _rglru_kernel(g_ref, x_ref, cw_ref, cb_ref, wa_ref, ba_ref, wi_ref, bi_ref, lam_ref,
                  y_ref, xpad_ref, carry_ref):
    rows, width = x_ref.shape
    n_blk = wa_ref.shape[0]
    blk = wa_ref.shape[1]
    first = pl.program_id(2) == 0

    @pl.when(first)
    def _():
        carry_ref[...] = jnp.zeros(carry_ref.shape, F32)

    xc = _causal_conv(x_ref, xpad_ref, cw_ref, cb_ref, first, LRU_CONV)
    xcb = xc.astype(BF16)
    ra = jnp.concatenate([jnp.dot(xcb[:, k * blk:(k + 1) * blk], wa_ref[k], preferred_element_type=F32)
                          for k in range(n_blk)], axis=1)
    ri = jnp.concatenate([jnp.dot(xcb[:, k * blk:(k + 1) * blk], wi_ref[k], preferred_element_type=F32)
                          for k in range(n_blk)], axis=1)
    r = _sigmoid(ra + ba_ref[...])
    i = _sigmoid(ri + bi_ref[...])
    log_a = (-LRU_C) * r * _softplus(-lam_ref[...])
    a = jnp.exp(log_a)
    q = -jnp.tanh(log_a) * (a * a + 1.0)
    u = jnp.where(q > 0.0, q * lax.rsqrt(q), 0.0) * (i * xc)

    row_in_tile = lax.broadcasted_iota(I32, (rows, width), 0) & (SUBLANES - 1)
    shift = 1
    while shift < SUBLANES:
        keep = row_in_tile >= shift
        a_prev = jnp.where(keep, pltpu.roll(a, shift, 0), 1.0)
        u_prev = jnp.where(keep, pltpu.roll(u, shift, 0), 0.0)
        u = a * u_prev + u
        a = a * a_prev
        shift *= 2
    carry = carry_ref[0:1, :]
    tiles = []
    for g in range(rows // SUBLANES):
        rs = slice(g * SUBLANES, (g + 1) * SUBLANES)
        h_g = u[rs] + a[rs] * carry
        tiles.append(h_g)
        carry = h_g[SUBLANES - 1:SUBLANES, :]
    h = jnp.concatenate(tiles, axis=0)
    carry_ref[...] = jnp.broadcast_to(carry, carry_ref.shape)

    gv = g_ref[...].astype(F32)
    gelu = 0.5 * gv * (1.0 + jnp.tanh(math.sqrt(2.0 / math.pi) * (gv + 0.044715 * (gv * gv * gv))))
    y_ref[...] = (h * gelu).astype(y_ref.dtype)


def _rglru_branch(gx, conv_w, conv_b, w_a, b_a, w_i, b_i, lam, *, batch, seq, chunk=256, cw=512):
    t = gx.shape[0]
    width = conv_w.shape[1]
    n_blocks, blk, _ = w_a.shape
    cw = _tile(width, cw)
    assert cw % blk == 0
    bpc = cw // blk
    chunk = _tile(seq, chunk)
    nc = seq // chunk
    row = lambda b, j, c: (b * nc + c, j)
    col = lambda b, j, c: (0, j)
    wblk = lambda b, j, c: (j, 0, 0)
    vec = lambda v: v.astype(F32).reshape(1, width)
    return pl.pallas_call(
        _rglru_kernel,
        grid=(batch, width // cw, nc),
        in_specs=[pl.BlockSpec((chunk, cw), row),
                  pl.BlockSpec((chunk, cw), lambda b, j, c: (b * nc + c, width // cw + j)),
                  pl.BlockSpec((LRU_CONV, cw), col),
                  pl.BlockSpec((1, cw), col),
                  pl.BlockSpec((bpc, blk, blk), wblk),
                  pl.BlockSpec((1, cw), col),
                  pl.BlockSpec((bpc, blk, blk), wblk),
                  pl.BlockSpec((1, cw), col),
                  pl.BlockSpec((1, cw), col)],
        out_specs=pl.BlockSpec((chunk, cw), row),
        out_shape=jax.ShapeDtypeStruct((t, width), BF16),
        scratch_shapes=[pltpu.VMEM((chunk + SUBLANES, cw), F32), pltpu.VMEM((SUBLANES, cw), F32)],
        compiler_params=_params("arbitrary", "arbitrary", "arbitrary"),
        name="rglru_scan",
    )(gx, gx, conv_w.astype(F32), vec(conv_b), w_a.astype(BF16), vec(b_a), w_i.astype(BF16), vec(b_i),
      vec(lam))


def _xattn_kernel(q_ref, k_ref, v_ref, o_ref, *, n_heads):
    width = q_ref.shape[1]
    hd = width // n_heads
    scale = hd ** -0.5
    for h in range(n_heads):
        q = q_ref[:, h * hd:(h + 1) * hd]
        k = k_ref[:, h * hd:(h + 1) * hd]
        v = v_ref[:, h * hd:(h + 1) * hd]
        s = lax.dot_general(q, k, (((1,), (1,)), ((), ())), preferred_element_type=F32) * scale
        s = s - jnp.max(s, axis=-1, keepdims=True)
        p = jnp.exp(s)
        p = p / jnp.sum(p, axis=-1, keepdims=True)
        o = jnp.dot(p.astype(BF16), v, preferred_element_type=F32)
        o_ref[:, h * hd:(h + 1) * hd] = o.astype(o_ref.dtype)


def _memory_xattn(q_all, q_block, kv, *, batch, seq, n_mem, tq=512):
    t = q_all.shape[0]
    width = kv.shape[1] // 2
    tq = _tile(seq, tq)
    nq = seq // tq
    return pl.pallas_call(
        functools.partial(_xattn_kernel, n_heads=MEM_HEADS),
        grid=(batch, nq),
        in_specs=[pl.BlockSpec((tq, width), lambda b, i: (b * nq + i, q_block)),
                  pl.BlockSpec((n_mem, width), lambda b, i: (b, 0)),
                  pl.BlockSpec((n_mem, width), lambda b, i: (b, 1))],
        out_specs=pl.BlockSpec((tq, width), lambda b, i: (b * nq + i, 0)),
        out_shape=jax.ShapeDtypeStruct((t, width), BF16),
        compiler_params=_params("parallel", "parallel"),
        name="mem_xattn",
    )(q_all, kv, kv)


def _merge_kernel(ys_ref, yl_ref, ym_ref, ws_ref, wl_ref, wm_ref, g0_ref, g1_ref, g2_ref, bg_ref, o_ref):
    acc = None
    for k, (y_ref, w_ref, g_ref) in enumerate(((ys_ref, ws_ref, g0_ref), (yl_ref, wl_ref, g1_ref),
                                                (ym_ref, wm_ref, g2_ref))):
        gate = _sigmoid(g_ref[...].astype(F32) + bg_ref[k:k + 1, :])
        term = gate * jnp.dot(y_ref[...], w_ref[...], preferred_element_type=F32)
        acc = term if acc is None else acc + term
    o_ref[...] = acc.astype(o_ref.dtype)


def _merge(y_ssd, y_lru, y_mem, w_ssd, w_lru, w_mem, gate_pre, b_gate, *, tm=1024, tn=512):
    t, kdim = y_ssd.shape
    d = w_ssd.shape[1]
    tm, tn = _tile(t, tm), _tile(d, tn)
    nj = d // tn
    yspec = pl.BlockSpec((tm, kdim), lambda i, j: (i, 0))
    wspec = pl.BlockSpec((kdim, tn), lambda i, j: (0, j))
    gspec = lambda k: pl.BlockSpec((tm, tn), lambda i, j: (i, k * nj + j))
    return pl.pallas_call(
        _merge_kernel,
        grid=(t // tm, nj),
        in_specs=[yspec, yspec, yspec, wspec, wspec, wspec, gspec(0), gspec(1), gspec(2),
                  pl.BlockSpec((b_gate.shape[0], tn), lambda i, j: (0, j))],
        out_specs=pl.BlockSpec((tm, tn), lambda i, j: (i, j)),
        out_shape=jax.ShapeDtypeStruct((t, d), BF16),
        compiler_params=_params("parallel", "parallel"),
        name="branch_merge",
    )(y_ssd, y_lru, y_mem, w_ssd, w_lru, w_mem, gate_pre, gate_pre, gate_pre, b_gate)


def _layernorm_kernel(x_ref, g_ref, b_ref, o_ref, *rest):
    x = x_ref[...]
    mu = jnp.mean(x, axis=-1, keepdims=True)
    xc = x - mu
    var = jnp.mean(xc * xc, axis=-1, keepdims=True)
    y = xc * lax.rsqrt(var + LN_EPS) * g_ref[...] + b_ref[...]
    o_ref[...] = y
    if rest:
        rest[0][...] = y.astype(BF16)


def _layernorm(x, g, b, *, with_bf16, tm=256):
    t, d = x.shape
    tm = _tile(t, tm)
    spec = pl.BlockSpec((tm, d), lambda i: (i, 0))
    vspec = pl.BlockSpec((1, d), lambda i: (0, 0))
    out_shape = [jax.ShapeDtypeStruct((t, d), F32)]
    out_specs = [spec]
    if with_bf16:
        out_shape.append(jax.ShapeDtypeStruct((t, d), BF16))
        out_specs.append(spec)
    return pl.pallas_call(
        _layernorm_kernel,
        grid=(t // tm,),
        in_specs=[spec, vspec, vspec],
        out_specs=out_specs,
        out_shape=out_shape,
        compiler_params=_params("parallel"),
        name="layernorm",
    )(x, g.astype(F32).reshape(1, d), b.astype(F32).reshape(1, d))


def _extract_max(vals, ids):
    m = jnp.max(vals, axis=0, keepdims=True)
    sel = jnp.min(jnp.where(vals == m, ids, ID_SENTINEL), axis=0, keepdims=True)
    return m, sel, ids == sel


def _row_ids(shape, scale=1, offset=0):
    return ((lax.broadcasted_iota(I32, shape, 0) + offset) * scale).astype(F32)


def _route_scores(q, keys_ref):
    half = keys_ref.shape[3]
    return [lax.dot_general(keys_ref[0, side], q[:, side * half:(side + 1) * half], (((1,), (1,)), ((), ())),
                            preferred_element_type=F32) for side in range(2)]


def _route_select(scores, hi_ref, lo_ref, gate_ref, ts_ref, ti_ref, bs_ref):
    n_keys, tn = scores[0].shape
    k_top = PEER_TOPK
    key_id = _row_ids((n_keys, tn))

    for side in range(2):
        s = scores[side]
        for k in range(k_top):
            m, sel, hit = _extract_max(s, key_id)
            ts_ref[side, k:k + 1, :] = m
            ti_ref[side, k:k + 1, :] = sel
            s = jnp.where(hit, NEG_INF, s)

    s0, s1 = ts_ref[0], ts_ref[1]
    i0, i1 = ti_ref[0], ti_ref[1]
    half_k = k_top // 2
    cand = [s0[0:1] + s1]
    a_id = [jnp.broadcast_to(i0[0:1], (k_top, tn))]
    b_id = [i1]
    pos = [_row_ids((k_top, tn))]
    for i in range(1, half_k):
        cand.append(s0[i:i + 1] + s1[0:half_k])
        a_id.append(jnp.broadcast_to(i0[i:i + 1], (half_k, tn)))
        b_id.append(i1[0:half_k])
        pos.append(_row_ids((half_k, tn), offset=i * k_top))
    cand.append(s0[half_k:k_top] + s1[0:1])
    a_id.append(i0[half_k:k_top])
    b_id.append(jnp.broadcast_to(i1[0:1], (k_top - half_k, tn)))
    pos.append(_row_ids((k_top - half_k, tn), scale=k_top, offset=half_k))
    cand = jnp.concatenate(cand, axis=0)
    a_id = jnp.concatenate(a_id, axis=0)
    b_id = jnp.concatenate(b_id, axis=0)
    pos = jnp.concatenate(pos, axis=0)

    for k in range(k_top):
        m, _, hit = _extract_max(cand, pos)
        bs_ref[k:k + 1, :] = m
        hi_ref[k:k + 1, :] = jnp.sum(jnp.where(hit, a_id, 0.0), axis=0, keepdims=True).astype(I32)
        lo_ref[k:k + 1, :] = jnp.sum(jnp.where(hit, b_id, 0.0), axis=0, keepdims=True).astype(I32)
        cand = jnp.where(hit, NEG_INF, cand)

    bs = bs_ref[...]
    e = jnp.exp(bs - bs[0:1])
    gate_ref[...] = e / jnp.sum(e, axis=0, keepdims=True)


def _peer_route_kernel(q_ref, keys_ref, hi_ref, lo_ref, gate_ref, ts_ref, ti_ref, bs_ref):
    _route_select(_route_scores(q_ref[...], keys_ref), hi_ref, lo_ref, gate_ref, ts_ref, ti_ref, bs_ref)


def _peer_route(q, keys, *, tn=1024):
    t = q.shape[0]
    n_heads, _, n_keys, half = keys.shape
    assert n_keys == LANES and PEER_TOPK % SUBLANES == 0
    tn = _tile(t, tn)
    slots = n_heads * PEER_TOPK
    ospec = pl.BlockSpec((PEER_TOPK, tn), lambda i, h: (h, i))
    return pl.pallas_call(
        _peer_route_kernel,
        grid=(t // tn, n_heads),
        in_specs=[pl.BlockSpec((tn, 2 * half), lambda i, h: (i, h)),
                  pl.BlockSpec((1, 2, n_keys, half), lambda i, h: (h, 0, 0, 0))],
        out_specs=[ospec, ospec, ospec],
        out_shape=[jax.ShapeDtypeStruct((slots, t), I32), jax.ShapeDtypeStruct((slots, t), I32),
                   jax.ShapeDtypeStruct((slots, t), F32)],
        scratch_shapes=[pltpu.VMEM((2, PEER_TOPK, tn), F32), pltpu.VMEM((2, PEER_TOPK, tn), F32),
                        pltpu.VMEM((PEER_TOPK, tn), F32)],
        compiler_params=_params("parallel", "parallel"),
        name="peer_route",
    )(q, keys)


def _erf_gelu(v):
    return 0.5 * v * (1.0 + lax.erf(v * (2.0 ** -0.5)))


def _peer_act_kernel(h_ref, u_ref, hi_ref, lo_ref, gate_ref, w_ref, acc_ref, *, n_chunks):
    j = pl.program_id(1)

    @pl.when(j == 0)
    def _():
        acc_ref[...] = jnp.zeros(acc_ref.shape, F32)

    dense = lax.dot_general(h_ref[...], u_ref[...], (((1,), (1,)), ((), ())),
                            preferred_element_type=F32)
    hi = hi_ref[...]
    lo = lo_ref[...]
    acc = acc_ref[...]
    per_step = dense.shape[1] // LANES
    for c in range(per_step):
        picked = jnp.take_along_axis(dense[:, c * LANES:(c + 1) * LANES], lo, axis=1)
        acc = jnp.where(hi == j * per_step + c, picked, acc)
    acc_ref[...] = acc

    @pl.when(j == n_chunks - 1)
    def _():
        w_ref[...] = gate_ref[...] * _erf_gelu(acc_ref[...])


def _peer_act(h_bf, u_tab, hi_t, lo_t, gate_t, *, tm=1024, te=1024):
    t, d = h_bf.shape
    n_exp = u_tab.shape[0]
    slots = hi_t.shape[1]
    assert slots == LANES
    tm, te = _tile(t, tm), _tile(n_exp, te)
    n_chunks = n_exp // te
    sspec = pl.BlockSpec((tm, slots), lambda i, j: (i, 0))
    return pl.pallas_call(
        functools.partial(_peer_act_kernel, n_chunks=n_chunks),
        grid=(t // tm, n_chunks),
        in_specs=[pl.BlockSpec((tm, d), lambda i, j: (i, 0)),
                  pl.BlockSpec((te, d), lambda i, j: (j, 0)),
                  sspec, sspec, sspec],
        out_specs=sspec,
        out_shape=jax.ShapeDtypeStruct((t, slots), F32),
        scratch_shapes=[pltpu.VMEM((tm, slots), F32)],
        compiler_params=_params("parallel", "arbitrary"),
        name="peer_act",
    )(h_bf, u_tab, hi_t, lo_t, gate_t)


def _peer_scatter_kernel(hi_ref, lo_ref, w_ref, o_ref, w3_ref, *, n_keys, pitch):
    tb, slots = hi_ref.shape
    row = lax.broadcasted_iota(I32, (tb, n_keys, slots), 1)
    hi = hi_ref[...][:, None, :]
    lo = lo_ref[...][:, None, :]
    wv = w_ref[...][:, None, :]
    a_t = jnp.where(row == hi, wv, 0.0).astype(BF16)
    r_t = jnp.where(row == lo, 1.0, 0.0).astype(BF16)
    w3 = lax.dot_general(a_t, r_t, (((2,), (2,)), ((0,), (0,))), preferred_element_type=F32)
    for t in range(tb):
        w3_ref[t * pitch:t * pitch + n_keys, :] = w3[t]
    for k in range(n_keys):
        o_ref[:, k * n_keys:(k + 1) * n_keys] = w3_ref[pl.ds(k, tb, stride=pitch), :].astype(o_ref.dtype)


def _peer_scatter(hi_t, lo_t, w_t, n_keys, *, tb=128):
    t, slots = hi_t.shape
    assert n_keys == LANES
    tb = _tile(t, tb, align=SUBLANES)
    pitch = n_keys + SUBLANES
    sspec = pl.BlockSpec((tb, slots), lambda i: (i, 0))
    return pl.pallas_call(
        functools.partial(_peer_scatter_kernel, n_keys=n_keys, pitch=pitch),
        grid=(t // tb,),
        in_specs=[sspec, sspec, sspec],
        out_specs=pl.BlockSpec((tb, n_keys * n_keys), lambda i: (i, 0)),
        out_shape=jax.ShapeDtypeStruct((t, n_keys * n_keys), BF16),
        scratch_shapes=[pltpu.VMEM((tb * pitch, n_keys), F32)],
        compiler_params=_params("parallel"),
        name="peer_scatter",
    )(hi_t, lo_t, w_t)


def _layer(h, mem, w_in, b_gate, ssd_conv_w, ssd_conv_b, ssd_dt_bias, ssd_a_log, ssd_d, ssd_norm_w,
           lru_conv_w, lru_conv_b, lru_w_a, lru_b_a, lru_w_i, lru_b_i, lru_lambda,
           mem_w_kv, w_branch_ssd, w_branch_lru, w_branch_mem, w_out, ln1_g, ln1_b,
           peer_w_q, peer_keys, peer_u, peer_v, ln2_g, ln2_b, *, depth):
    bsz, seq, d = h.shape
    t = bsz * seq
    n_mem = mem.shape[1]
    alpha = (2.0 * depth) ** 0.25

    d_inner = w_branch_ssd.shape[0]
    xbc_w = ssd_conv_w.shape[1]
    n_heads = ssd_dt_bias.shape[0]
    lru_w = w_branch_lru.shape[0]
    mem_w = w_branch_mem.shape[0]
    n_branch = b_gate.shape[0]
    c_dt = d_inner + xbc_w
    c_lru = c_dt + n_heads
    c_gate = c_lru + 2 * lru_w + mem_w
    assert w_in.shape[1] == c_gate + n_branch * d and n_branch == 3

    hf = h.reshape(t, d)
    h_bf = hf.astype(BF16)
    assert lru_w == mem_w
    w_t = w_in.T
    w_xz = jnp.concatenate([w_t[d_inner:c_dt], w_t[:d_inner]], axis=0).astype(BF16)
    w_dt = jnp.pad(w_t[c_dt:c_lru], ((0, LANES - n_heads), (0, 0))).astype(BF16)
    w_lm = w_t[c_lru:c_gate].astype(BF16)
    w_g = w_t[c_gate:].astype(BF16)

    xz = _matmul(h_bf, w_xz, BF16, b_is_transposed=True, name="proj_xz")
    dt_pad = _matmul(h_bf, w_dt, F32, b_is_transposed=True, name="proj_dt")
    lm = _matmul(h_bf, w_lm, BF16, b_is_transposed=True, name="proj_lru_mem")
    gate_pre = _matmul(h_bf, w_g, BF16, b_is_transposed=True, name="proj_gates")

    y_ssd = _ssd_branch(xz, dt_pad, ssd_conv_w, ssd_conv_b, ssd_dt_bias,
                        ssd_a_log, ssd_d, ssd_norm_w, batch=bsz, seq=seq)
    y_lru = _rglru_branch(lm, lru_conv_w, lru_conv_b, lru_w_a, lru_b_a,
                          lru_w_i, lru_b_i, lru_lambda, batch=bsz, seq=seq)
    kv = _matmul(mem.reshape(bsz * n_mem, d).astype(BF16), mem_w_kv.astype(BF16), BF16, name="mem_kv")
    y_mem = _memory_xattn(lm, 2, kv, batch=bsz, seq=seq, n_mem=n_mem)

    merged = _merge(y_ssd, y_lru, y_mem, w_branch_ssd.astype(BF16), w_branch_lru.astype(BF16),
                    w_branch_mem.astype(BF16), gate_pre, b_gate.astype(F32))
    pre1 = _matmul(merged, w_out.astype(BF16), F32, tn=512, res=hf, res_scale=alpha, name="out_proj")
    h1, h1_bf = _layernorm(pre1, ln1_g, ln1_b, with_bf16=True)

    n_keys = peer_keys.shape[2]
    q = _matmul(h1_bf, peer_w_q.astype(BF16), BF16, name="peer_q")
    hi, lo, gate = _peer_route(q, peer_keys.astype(BF16))
    hi_t, lo_t, gate_t = hi.T, lo.T, gate.T
    w_t = _peer_act(h1_bf, peer_u.astype(BF16), hi_t, lo_t, gate_t)
    dense_w = _peer_scatter(hi_t, lo_t, w_t, n_keys)
    pre2 = _matmul(dense_w, peer_v.astype(BF16), F32, tm=512, tn=256, res=h1, res_scale=alpha, name="peer_v")
    (out,) = _layernorm(pre2, ln2_g, ln2_b, with_bf16=False)
    return out.reshape(bsz, seq, d)


def kernel(x, mem, w_in, b_gate, ssd_conv_w, ssd_conv_b, ssd_dt_bias, ssd_a_log, ssd_d, ssd_norm_w, lru_conv_w, lru_conv_b, lru_w_a, lru_b_a, lru_w_i, lru_b_i, lru_lambda, mem_w_kv, w_branch_ssd, w_branch_lru, w_branch_mem, w_out, ln1_g, ln1_b, peer_w_q, peer_keys, peer_u, peer_v, ln2_g, ln2_b):
    params = (w_in, b_gate, ssd_conv_w, ssd_conv_b, ssd_dt_bias, ssd_a_log, ssd_d, ssd_norm_w, lru_conv_w,
              lru_conv_b, lru_w_a, lru_b_a, lru_w_i, lru_b_i, lru_lambda, mem_w_kv, w_branch_ssd, w_branch_lru,
              w_branch_mem, w_out, ln1_g, ln1_b, peer_w_q, peer_keys, peer_u, peer_v, ln2_g, ln2_b)
    depth = w_in.shape[0]
    h = x
    for l in range(depth):
        h = _layer(h, mem, *(p[l] for p in params), depth=depth)
    return h
```

```python
import functools
import math

import jax
import jax.numpy as jnp
from jax import lax
from jax.experimental import pallas as pl
from jax.experimental.pallas import tpu as pltpu

F32 = jnp.float32
BF16 = jnp.bfloat16
I32 = jnp.int32

SSD_HEAD_DIM = 64
SSD_D_STATE = 128
SSD_CONV = 4
LRU_CONV = 4
LRU_C = 8.0
MEM_HEADS = 4
PEER_TOPK = 16
LN_EPS = 1e-5
RMS_EPS = 1e-5

LANES = 128
SUBLANES = 8
VMEM_LIMIT_BYTES = 56 * 1024 * 1024

HIGHEST = lax.Precision.HIGHEST
NEG_INF = float("-inf")
ID_SENTINEL = 1e9


def _params(*semantics):
    return pltpu.CompilerParams(dimension_semantics=semantics, vmem_limit_bytes=VMEM_LIMIT_BYTES)


def _tile(dim, want, align=LANES):
    if dim <= want:
        return dim
    t = want - want % align
    while t > align and dim % t:
        t -= align
    assert dim % t == 0, (dim, want)
    return t


def _sigmoid(v):
    return 1.0 / (1.0 + jnp.exp(-v))


def _softplus(v):
    return jnp.maximum(v, 0.0) + jnp.log1p(jnp.exp(-jnp.abs(v)))


def _split3(v):
    hi = v.astype(BF16)
    rest = v - hi.astype(F32)
    mid = rest.astype(BF16)
    lo = (rest - mid.astype(F32)).astype(BF16)
    return hi, mid, lo


def _matmul_kernel(a_ref, b_ref, *rest, res_scale, b_is_transposed):
    o_ref = rest[-1]
    contract_b = 1 if b_is_transposed else 0
    acc = lax.dot_general(a_ref[...], b_ref[...], (((1,), (contract_b,)), ((), ())), preferred_element_type=F32)
    if len(rest) == 2:
        acc = acc + res_scale * rest[0][...].astype(F32)
    o_ref[...] = acc.astype(o_ref.dtype)


def _matmul(a, b, out_dtype, *, tm=1024, tn=1024, b_is_transposed=False, res=None, res_scale=1.0, name="matmul"):
    m, kdim = a.shape
    n = b.shape[0] if b_is_transposed else b.shape[1]
    tm, tn = _tile(m, tm), _tile(n, tn)
    b_spec = (pl.BlockSpec((tn, kdim), lambda i, j: (j, 0)) if b_is_transposed
              else pl.BlockSpec((kdim, tn), lambda i, j: (0, j)))
    in_specs = [pl.BlockSpec((tm, kdim), lambda i, j: (i, 0)), b_spec]
    args = [a, b]
    if res is not None:
        in_specs.append(pl.BlockSpec((tm, tn), lambda i, j: (i, j)))
        args.append(res)
    return pl.pallas_call(
        functools.partial(_matmul_kernel, res_scale=res_scale, b_is_transposed=b_is_transposed),
        grid=(m // tm, n // tn),
        in_specs=in_specs,
        out_specs=pl.BlockSpec((tm, tn), lambda i, j: (i, j)),
        out_shape=jax.ShapeDtypeStruct((m, n), out_dtype),
        compiler_params=_params("parallel", "parallel"),
        name=name,
    )(*args)


def _causal_conv(x_ref, xpad_ref, w_ref, b_ref, first, n_taps):
    rows = x_ref.shape[0]
    x = x_ref[...].astype(F32)

    @pl.when(first)
    def _():
        xpad_ref[0:SUBLANES, :] = jnp.zeros((SUBLANES, x.shape[1]), F32)

    xpad_ref[SUBLANES:SUBLANES + rows, :] = x
    w = w_ref[...]
    y = b_ref[...] + w[n_taps - 1:n_taps, :] * x
    for back in range(1, n_taps):
        tap = n_taps - 1 - back
        y = y + w[tap:tap + 1, :] * xpad_ref[SUBLANES - back:SUBLANES - back + rows, :]
    xpad_ref[0:SUBLANES, :] = x[rows - SUBLANES:rows, :]
    return y


def _ssd_kernel(z_ref, xbc_ref, dt_ref, cw_ref, cb_ref, dtb_ref, alog_ref, dskip_ref, nw_ref, e_ref,
                y_ref, xpad_ref, state_ref, *, n_groups, heads_per_group):
    rows = z_ref.shape[0]
    d_inner = z_ref.shape[1]
    gn = n_groups * SSD_D_STATE
    gw = heads_per_group * SSD_HEAD_DIM
    first = pl.program_id(1) == 0

    @pl.when(first)
    def _():
        state_ref[...] = jnp.zeros(state_ref.shape, F32)

    conv = _causal_conv(xbc_ref, xpad_ref, cw_ref, cb_ref, first, SSD_CONV)
    act = conv * _sigmoid(conv)

    dtv = _softplus(dt_ref[...] + dtb_ref[...])
    da = dtv * (-jnp.exp(alog_ref[...]))
    r_i = lax.broadcasted_iota(I32, (rows, rows), 0)
    c_i = lax.broadcasted_iota(I32, (rows, rows), 1)
    causal = r_i >= c_i
    cs = jnp.dot(causal.astype(F32), da, precision=HIGHEST, preferred_element_type=F32)
    cs_end = cs[rows - 1:rows, :]
    cs_t = cs.T

    stack = jnp.concatenate([dtv, jnp.exp(cs_end - cs), jnp.exp(cs)], axis=0)
    ex = jnp.dot(jnp.concatenate(_split3(stack), axis=1), e_ref[...], preferred_element_type=F32)
    dt_x, dte_x, ecs_x = ex[0:rows], ex[rows:2 * rows], ex[2 * rows:3 * rows]

    xs = act[:, 0:d_inner]
    xdt = xs * dt_x
    xdt_end = xdt * dte_x
    lane = lax.broadcasted_iota(I32, (1, gw), 1)

    for g in range(n_groups):
        lo, hi = g * gw, (g + 1) * gw
        b_f = act[:, d_inner + g * SSD_D_STATE:d_inner + (g + 1) * SSD_D_STATE]
        c_g = act[:, d_inner + gn + g * SSD_D_STATE:d_inner + gn + (g + 1) * SSD_D_STATE].astype(BF16)
        b_g = b_f.astype(BF16)
        cb = lax.dot_general(c_g, b_g, (((1,), (1,)), ((), ())), preferred_element_type=F32)
        st = state_ref[g]
        y_off = jnp.dot(c_g, st.astype(BF16), preferred_element_type=F32) * ecs_x[:, lo:hi]
        xdt_g = xdt[:, lo:hi]
        y_diag = jnp.zeros((rows, gw), F32)
        for r in range(heads_per_group):
            h = g * heads_per_group + r
            seg = cs[:, h:h + 1] - cs_t[h:h + 1, :]
            decay = jnp.where(causal, jnp.exp(jnp.where(causal, seg, 0.0)), 0.0)
            m_h = (cb * decay).astype(BF16)
            in_head = (lane >= r * SSD_HEAD_DIM) & (lane < (r + 1) * SSD_HEAD_DIM)
            x_h = jnp.where(in_head, xdt_g, 0.0).astype(BF16)
            y_diag = y_diag + jnp.dot(m_h, x_h, preferred_element_type=F32)
        new_st = st * ecs_x[rows - 1:rows, lo:hi] + jnp.dot(
            b_f.T.astype(BF16), xdt_end[:, lo:hi].astype(BF16), preferred_element_type=F32)
        state_ref[g] = new_st

        y = y_diag + y_off + xs[:, lo:hi] * dskip_ref[:, lo:hi]
        zg = z_ref[:, lo:hi].astype(F32)
        y = y * (zg * _sigmoid(zg))
        ms = jnp.mean(y * y, axis=-1, keepdims=True)
        y = y * lax.rsqrt(ms + RMS_EPS) * nw_ref[:, lo:hi]
        y_ref[:, lo:hi] = y.astype(y_ref.dtype)


def _ssd_branch(xz, dt_pad, conv_w, conv_b, dt_bias, a_log, d_skip, norm_w, *, batch, seq, chunk=128):
    t = xz.shape[0]
    xbc_w = conv_w.shape[1]
    d_inner = xz.shape[1] - xbc_w
    assert xbc_w % d_inner == 0
    n_heads = dt_bias.shape[0]
    n_groups = (xbc_w - d_inner) // (2 * SSD_D_STATE)
    hpg = n_heads // n_groups
    assert d_inner == n_heads * SSD_HEAD_DIM and d_inner // n_groups == hpg * SSD_HEAD_DIM
    assert n_heads <= LANES
    chunk = _tile(seq, chunk)
    nc = seq // chunk
    pad = LANES - n_heads
    dtb = jnp.pad(dt_bias.astype(F32), (0, pad)).reshape(1, LANES)
    alog = jnp.pad(a_log.astype(F32), (0, pad)).reshape(1, LANES)
    dskip = jnp.repeat(d_skip.astype(F32), SSD_HEAD_DIM).reshape(1, d_inner)
    expand = (jnp.arange(LANES)[:, None] == (jnp.arange(d_inner)[None, :] // SSD_HEAD_DIM)).astype(BF16)
    expand = jnp.concatenate([expand] * 3, axis=0)
    row = lambda b, c: (b * nc + c, 0)
    fixed = lambda b, c: (0, 0)
    return pl.pallas_call(
        functools.partial(_ssd_kernel, n_groups=n_groups, heads_per_group=hpg),
        grid=(batch, nc),
        in_specs=[pl.BlockSpec((chunk, d_inner), lambda b, c: (b * nc + c, xbc_w // d_inner)),
                  pl.BlockSpec((chunk, xbc_w), row),
                  pl.BlockSpec((chunk, LANES), row),
                  pl.BlockSpec((SSD_CONV, xbc_w), fixed),
                  pl.BlockSpec((1, xbc_w), fixed),
                  pl.BlockSpec((1, LANES), fixed),
                  pl.BlockSpec((1, LANES), fixed),
                  pl.BlockSpec((1, d_inner), fixed),
                  pl.BlockSpec((1, d_inner), fixed),
                  pl.BlockSpec((3 * LANES, d_inner), fixed)],
        out_specs=pl.BlockSpec((chunk, d_inner), row),
        out_shape=jax.ShapeDtypeStruct((t, d_inner), BF16),
        scratch_shapes=[pltpu.VMEM((chunk + SUBLANES, xbc_w), F32),
                        pltpu.VMEM((n_groups, SSD_D_STATE, hpg * SSD_HEAD_DIM), F32)],
        compiler_params=_params("arbitrary", "arbitrary"),
        name="ssd_scan",
    )(xz, xz, dt_pad, conv_w.astype(F32), conv_b.astype(F32).reshape(1, xbc_w), dtb, alog, dskip,
      norm_w.astype(F32).reshape(1, d_inner), expand)


def _rglru_kernel(g_ref, x_ref, cw_ref, cb_ref, wa_ref, ba_ref, wi_ref, bi_ref, lam_ref,
                  y_ref, xpad_ref, carry_ref):
    rows, width = x_ref.shape
    n_blk = wa_ref.shape[0]
    blk = wa_ref.shape[1]
    first = pl.program_id(2) == 0

    @pl.when(first)
    def _():
        carry_ref[...] = jnp.zeros(carry_ref.shape, F32)

    xc = _causal_conv(x_ref, xpad_ref, cw_ref, cb_ref, first, LRU_CONV)
    xcb = xc.astype(BF16)
    ra = jnp.concatenate([jnp.dot(xcb[:, k * blk:(k + 1) * blk], wa_ref[k], preferred_element_type=F32)
                          for k in range(n_blk)], axis=1)
    ri = jnp.concatenate([jnp.dot(xcb[:, k * blk:(k + 1) * blk], wi_ref[k], preferred_element_type=F32)
                          for k in range(n_blk)], axis=1)
    r = _sigmoid(ra + ba_ref[...])
    i = _sigmoid(ri + bi_ref[...])
    log_a = (-LRU_C) * r * _softplus(-lam_ref[...])
    a = jnp.exp(log_a)
    q = -jnp.tanh(log_a) * (a * a + 1.0)
    u = jnp.where(q > 0.0, q * lax.rsqrt(q), 0.0) * (i * xc)

    row_in_tile = lax.broadcasted_iota(I32, (rows, width), 0) & (SUBLANES - 1)
    shift = 1
    while shift < SUBLANES:
        keep = row_in_tile >= shift
        a_prev = jnp.where(keep, pltpu.roll(a, shift, 0), 1.0)
        u_prev = jnp.where(keep, pltpu.roll(u, shift, 0), 0.0)
        u = a * u_prev + u
        a = a * a_prev
        shift *= 2
    carry = carry_ref[0:1, :]
    tiles = []
    for g in range(rows // SUBLANES):
        rs = slice(g * SUBLANES, (g + 1) * SUBLANES)
        h_g = u[rs] + a[rs] * carry
        tiles.append(h_g)
        carry = h_g[SUBLANES - 1:SUBLANES, :]
    h = jnp.concatenate(tiles, axis=0)
    carry_ref[...] = jnp.broadcast_to(carry, carry_ref.shape)

    gv = g_ref[...].astype(F32)
    gelu = 0.5 * gv * (1.0 + jnp.tanh(math.sqrt(2.0 / math.pi) * (gv + 0.044715 * (gv * gv * gv))))
    y_ref[...] = (h * gelu).astype(y_ref.dtype)


def _rglru_branch(gx, conv_w, conv_b, w_a, b_a, w_i, b_i, lam, *, batch, seq, chunk=512, cw=512):
    t = gx.shape[0]
    width = conv_w.shape[1]
    n_blocks, blk, _ = w_a.shape
    cw = _tile(width, cw)
    assert cw % blk == 0
    bpc = cw // blk
    chunk = _tile(seq, chunk)
    nc = seq // chunk
    row = lambda b, j, c: (b * nc + c, j)
    col = lambda b, j, c: (0, j)
    wblk = lambda b, j, c: (j, 0, 0)
    vec = lambda v: v.astype(F32).reshape(1, width)
    return pl.pallas_call(
        _rglru_kernel,
        grid=(batch, width // cw, nc),
        in_specs=[pl.BlockSpec((chunk, cw), row),
                  pl.BlockSpec((chunk, cw), lambda b, j, c: (b * nc + c, width // cw + j)),
                  pl.BlockSpec((LRU_CONV, cw), col),
                  pl.BlockSpec((1, cw), col),
                  pl.BlockSpec((bpc, blk, blk), wblk),
                  pl.BlockSpec((1, cw), col),
                  pl.BlockSpec((bpc, blk, blk), wblk),
                  pl.BlockSpec((1, cw), col),
                  pl.BlockSpec((1, cw), col)],
        out_specs=pl.BlockSpec((chunk, cw), row),
        out_shape=jax.ShapeDtypeStruct((t, width), BF16),
        scratch_shapes=[pltpu.VMEM((chunk + SUBLANES, cw), F32), pltpu.VMEM((SUBLANES, cw), F32)],
        compiler_params=_params("arbitrary", "arbitrary", "arbitrary"),
        name="rglru_scan",
    )(gx, gx, conv_w.astype(F32), vec(conv_b), w_a.astype(BF16), vec(b_a), w_i.astype(BF16), vec(b_i),
      vec(lam))


def _xattn_kernel(q_ref, k_ref, v_ref, o_ref, *, n_heads):
    width = q_ref.shape[1]
    hd = width // n_heads
    scale = hd ** -0.5
    for h in range(n_heads):
        q = q_ref[:, h * hd:(h + 1) * hd]
        k = k_ref[:, h * hd:(h + 1) * hd]
        v = v_ref[:, h * hd:(h + 1) * hd]
        s = lax.dot_general(q, k, (((1,), (1,)), ((), ())), preferred_element_type=F32) * scale
        s = s - jnp.max(s, axis=-1, keepdims=True)
        p = jnp.exp(s)
        p = p / jnp.sum(p, axis=-1, keepdims=True)
        o = jnp.dot(p.astype(BF16), v, preferred_element_type=F32)
        o_ref[:, h * hd:(h + 1) * hd] = o.astype(o_ref.dtype)


def _memory_xattn(q_all, q_block, kv, *, batch, seq, n_mem, tq=512):
    t = q_all.shape[0]
    width = kv.shape[1] // 2
    tq = _tile(seq, tq)
    nq = seq // tq
    return pl.pallas_call(
        functools.partial(_xattn_kernel, n_heads=MEM_HEADS),
        grid=(batch, nq),
        in_specs=[pl.BlockSpec((tq, width), lambda b, i: (b * nq + i, q_block)),
                  pl.BlockSpec((n_mem, width), lambda b, i: (b, 0)),
                  pl.BlockSpec((n_mem, width), lambda b, i: (b, 1))],
        out_specs=pl.BlockSpec((tq, width), lambda b, i: (b * nq + i, 0)),
        out_shape=jax.ShapeDtypeStruct((t, width), BF16),
        compiler_params=_params("parallel", "parallel"),
        name="mem_xattn",
    )(q_all, kv, kv)


def _merge_kernel(ys_ref, yl_ref, ym_ref, ws_ref, wl_ref, wm_ref, g0_ref, g1_ref, g2_ref, bg_ref, o_ref):
    acc = None
    for k, (y_ref, w_ref, g_ref) in enumerate(((ys_ref, ws_ref, g0_ref), (yl_ref, wl_ref, g1_ref),
                                                (ym_ref, wm_ref, g2_ref))):
        gate = _sigmoid(g_ref[...].astype(F32) + bg_ref[k:k + 1, :])
        term = gate * jnp.dot(y_ref[...], w_ref[...], preferred_element_type=F32)
        acc = term if acc is None else acc + term
    o_ref[...] = acc.astype(o_ref.dtype)


def _merge(y_ssd, y_lru, y_mem, w_ssd, w_lru, w_mem, gate_pre, b_gate, *, tm=1024, tn=512):
    t, kdim = y_ssd.shape
    d = w_ssd.shape[1]
    tm, tn = _tile(t, tm), _tile(d, tn)
    nj = d // tn
    yspec = pl.BlockSpec((tm, kdim), lambda i, j: (i, 0))
    wspec = pl.BlockSpec((kdim, tn), lambda i, j: (0, j))
    gspec = lambda k: pl.BlockSpec((tm, tn), lambda i, j: (i, k * nj + j))
    return pl.pallas_call(
        _merge_kernel,
        grid=(t // tm, nj),
        in_specs=[yspec, yspec, yspec, wspec, wspec, wspec, gspec(0), gspec(1), gspec(2),
                  pl.BlockSpec((b_gate.shape[0], tn), lambda i, j: (0, j))],
        out_specs=pl.BlockSpec((tm, tn), lambda i, j: (i, j)),
        out_shape=jax.ShapeDtypeStruct((t, d), BF16),
        compiler_params=_params("parallel", "parallel"),
        name="branch_merge",
    )(y_ssd, y_lru, y_mem, w_ssd, w_lru, w_mem, gate_pre, gate_pre, gate_pre, b_gate)


def _layernorm_kernel(x_ref, g_ref, b_ref, o_ref, *rest):
    x = x_ref[...]
    mu = jnp.mean(x, axis=-1, keepdims=True)
    xc = x - mu
    var = jnp.mean(xc * xc, axis=-1, keepdims=True)
    y = xc * lax.rsqrt(var + LN_EPS) * g_ref[...] + b_ref[...]
    o_ref[...] = y
    if rest:
        rest[0][...] = y.astype(BF16)


def _layernorm(x, g, b, *, with_bf16, tm=256):
    t, d = x.shape
    tm = _tile(t, tm)
    spec = pl.BlockSpec((tm, d), lambda i: (i, 0))
    vspec = pl.BlockSpec((1, d), lambda i: (0, 0))
    out_shape = [jax.ShapeDtypeStruct((t, d), F32)]
    out_specs = [spec]
    if with_bf16:
        out_shape.append(jax.ShapeDtypeStruct((t, d), BF16))
        out_specs.append(spec)
    return pl.pallas_call(
        _layernorm_kernel,
        grid=(t // tm,),
        in_specs=[spec, vspec, vspec],
        out_specs=out_specs,
        out_shape=out_shape,
        compiler_params=_params("parallel"),
        name="layernorm",
    )(x, g.astype(F32).reshape(1, d), b.astype(F32).reshape(1, d))


def _extract_max(vals, ids):
    m = jnp.max(vals, axis=0, keepdims=True)
    sel = jnp.min(jnp.where(vals == m, ids, ID_SENTINEL), axis=0, keepdims=True)
    return m, sel, ids == sel


def _row_ids(shape, scale=1, offset=0):
    return ((lax.broadcasted_iota(I32, shape, 0) + offset) * scale).astype(F32)


def _route_scores(q, keys_ref):
    half = keys_ref.shape[3]
    return [lax.dot_general(keys_ref[0, side], q[:, side * half:(side + 1) * half], (((1,), (1,)), ((), ())),
                            preferred_element_type=F32) for side in range(2)]


def _route_select(scores, hi_ref, lo_ref, gate_ref, ts_ref, ti_ref, bs_ref):
    n_keys, tn = scores[0].shape
    k_top = PEER_TOPK
    key_id = _row_ids((n_keys, tn))

    for side in range(2):
        s = scores[side]
        for k in range(k_top):
            m, sel, hit = _extract_max(s, key_id)
            ts_ref[side, k:k + 1, :] = m
            ti_ref[side, k:k + 1, :] = sel
            s = jnp.where(hit, NEG_INF, s)

    s0, s1 = ts_ref[0], ts_ref[1]
    i0, i1 = ti_ref[0], ti_ref[1]
    half_k = k_top // 2
    cand = [s0[0:1] + s1]
    a_id = [jnp.broadcast_to(i0[0:1], (k_top, tn))]
    b_id = [i1]
    pos = [_row_ids((k_top, tn))]
    for i in range(1, half_k):
        cand.append(s0[i:i + 1] + s1[0:half_k])
        a_id.append(jnp.broadcast_to(i0[i:i + 1], (half_k, tn)))
        b_id.append(i1[0:half_k])
        pos.append(_row_ids((half_k, tn), offset=i * k_top))
    cand.append(s0[half_k:k_top] + s1[0:1])
    a_id.append(i0[half_k:k_top])
    b_id.append(jnp.broadcast_to(i1[0:1], (k_top - half_k, tn)))
    pos.append(_row_ids((k_top - half_k, tn), scale=k_top, offset=half_k))
    cand = jnp.concatenate(cand, axis=0)
    a_id = jnp.concatenate(a_id, axis=0)
    b_id = jnp.concatenate(b_id, axis=0)
    pos = jnp.concatenate(pos, axis=0)

    for k in range(k_top):
        m, _, hit = _extract_max(cand, pos)
        bs_ref[k:k + 1, :] = m
        hi_ref[k:k + 1, :] = jnp.sum(jnp.where(hit, a_id, 0.0), axis=0, keepdims=True).astype(I32)
        lo_ref[k:k + 1, :] = jnp.sum(jnp.where(hit, b_id, 0.0), axis=0, keepdims=True).astype(I32)
        cand = jnp.where(hit, NEG_INF, cand)

    bs = bs_ref[...]
    e = jnp.exp(bs - bs[0:1])
    gate_ref[...] = e / jnp.sum(e, axis=0, keepdims=True)


def _peer_route_kernel(q_ref, keys_ref, hi_ref, lo_ref, gate_ref, ts_ref, ti_ref, bs_ref):
    _route_select(_route_scores(q_ref[...], keys_ref), hi_ref, lo_ref, gate_ref, ts_ref, ti_ref, bs_ref)


def _peer_route(q, keys, *, tn=1024):
    t = q.shape[0]
    n_heads, _, n_keys, half = keys.shape
    assert n_keys == LANES and PEER_TOPK % SUBLANES == 0
    tn = _tile(t, tn)
    slots = n_heads * PEER_TOPK
    ospec = pl.BlockSpec((PEER_TOPK, tn), lambda i, h: (h, i))
    return pl.pallas_call(
        _peer_route_kernel,
        grid=(t // tn, n_heads),
        in_specs=[pl.BlockSpec((tn, 2 * half), lambda i, h: (i, h)),
                  pl.BlockSpec((1, 2, n_keys, half), lambda i, h: (h, 0, 0, 0))],
        out_specs=[ospec, ospec, ospec],
        out_shape=[jax.ShapeDtypeStruct((slots, t), I32), jax.ShapeDtypeStruct((slots, t), I32),
                   jax.ShapeDtypeStruct((slots, t), F32)],
        scratch_shapes=[pltpu.VMEM((2, PEER_TOPK, tn), F32), pltpu.VMEM((2, PEER_TOPK, tn), F32),
                        pltpu.VMEM((PEER_TOPK, tn), F32)],
        compiler_params=_params("parallel", "parallel"),
        name="peer_route",
    )(q, keys)


def _erf_gelu(v):
    return 0.5 * v * (1.0 + lax.erf(v * (2.0 ** -0.5)))


def _pick_routed(dense, first_hi, hi, lo, acc):
    for c in range(dense.shape[1] // LANES):
        picked = jnp.take_along_axis(dense[:, c * LANES:(c + 1) * LANES], lo, axis=1)
        acc = jnp.where(hi == first_hi + c, picked, acc)
    return acc


def _peer_act_kernel(h_ref, u_ref, hi_ref, lo_ref, gate_ref, w_ref, acc_ref, dense_ref, *, n_chunks):
    i = pl.program_id(0)
    j = pl.program_id(1)
    per_chunk = dense_ref.shape[1] // LANES

    @pl.when((i == 0) & (j == 0))
    def _():
        dense_ref[...] = jnp.zeros(dense_ref.shape, F32)

    @pl.when(j == 0)
    def _():
        acc_ref[...] = jnp.zeros(acc_ref.shape, F32)

    hi = hi_ref[...]
    lo = lo_ref[...]
    acc_ref[...] = _pick_routed(dense_ref[...], (j - 1) * per_chunk, hi, lo, acc_ref[...])
    dense_ref[...] = lax.dot_general(h_ref[...], u_ref[...], (((1,), (1,)), ((), ())),
                                     preferred_element_type=F32)

    @pl.when(j == n_chunks - 1)
    def _():
        acc = _pick_routed(dense_ref[...], j * per_chunk, hi, lo, acc_ref[...])
        w_ref[...] = gate_ref[...] * _erf_gelu(acc)


def _peer_act(h_bf, u_tab, hi_t, lo_t, gate_t, *, tm=1024, te=1024):
    t, d = h_bf.shape
    n_exp = u_tab.shape[0]
    slots = hi_t.shape[1]
    assert slots == LANES
    tm, te = _tile(t, tm), _tile(n_exp, te)
    n_chunks = n_exp // te
    sspec = pl.BlockSpec((tm, slots), lambda i, j: (i, 0))
    return pl.pallas_call(
        functools.partial(_peer_act_kernel, n_chunks=n_chunks),
        grid=(t // tm, n_chunks),
        in_specs=[pl.BlockSpec((tm, d), lambda i, j: (i, 0)),
                  pl.BlockSpec((te, d), lambda i, j: (j, 0)),
                  sspec, sspec, sspec],
        out_specs=sspec,
        out_shape=jax.ShapeDtypeStruct((t, slots), F32),
        scratch_shapes=[pltpu.VMEM((tm, slots), F32), pltpu.VMEM((tm, te), F32)],
        compiler_params=_params("arbitrary", "arbitrary"),
        name="peer_act",
    )(h_bf, u_tab, hi_t, lo_t, gate_t)


def _peer_scatter_kernel(hi_ref, lo_ref, w_ref, o_ref, w3_ref, *, n_keys, pitch):
    tb, slots = hi_ref.shape
    row = lax.broadcasted_iota(I32, (tb, n_keys, slots), 1)
    hi = hi_ref[...][:, None, :]
    lo = lo_ref[...][:, None, :]
    wv = w_ref[...][:, None, :]
    a_t = jnp.where(row == hi, wv, 0.0).astype(BF16)
    r_t = jnp.where(row == lo, 1.0, 0.0).astype(BF16)
    w3 = lax.dot_general(a_t, r_t, (((2,), (2,)), ((0,), (0,))), preferred_element_type=F32)
    for t in range(tb):
        w3_ref[t * pitch:t * pitch + n_keys, :] = w3[t]
    for k in range(n_keys):
        o_ref[:, k * n_keys:(k + 1) * n_keys] = w3_ref[pl.ds(k, tb, stride=pitch), :].astype(o_ref.dtype)


def _peer_scatter(hi_t, lo_t, w_t, n_keys, *, tb=128):
    t, slots = hi_t.shape
    assert n_keys == LANES
    tb = _tile(t, tb, align=SUBLANES)
    pitch = n_keys + SUBLANES
    sspec = pl.BlockSpec((tb, slots), lambda i: (i, 0))
    return pl.pallas_call(
        functools.partial(_peer_scatter_kernel, n_keys=n_keys, pitch=pitch),
        grid=(t // tb,),
        in_specs=[sspec, sspec, sspec],
        out_specs=pl.BlockSpec((tb, n_keys * n_keys), lambda i: (i, 0)),
        out_shape=jax.ShapeDtypeStruct((t, n_keys * n_keys), BF16),
        scratch_shapes=[pltpu.VMEM((tb * pitch, n_keys), F32)],
        compiler_params=_params("parallel"),
        name="peer_scatter",
    )(hi_t, lo_t, w_t)


def _layer(h, mem, w_in, b_gate, ssd_conv_w, ssd_conv_b, ssd_dt_bias, ssd_a_log, ssd_d, ssd_norm_w,
           lru_conv_w, lru_conv_b, lru_w_a, lru_b_a, lru_w_i, lru_b_i, lru_lambda,
           mem_w_kv, w_branch_ssd, w_branch_lru, w_branch_mem, w_out, ln1_g, ln1_b,
           peer_w_q, peer_keys, peer_u, peer_v, ln2_g, ln2_b, *, depth):
    bsz, seq, d = h.shape
    t = bsz * seq
    n_mem = mem.shape[1]
    alpha = (2.0 * depth) ** 0.25

    d_inner = w_branch_ssd.shape[0]
    xbc_w = ssd_conv_w.shape[1]
    n_heads = ssd_dt_bias.shape[0]
    lru_w = w_branch_lru.shape[0]
    mem_w = w_branch_mem.shape[0]
    n_branch = b_gate.shape[0]
    c_dt = d_inner + xbc_w
    c_lru = c_dt + n_heads
    c_gate = c_lru + 2 * lru_w + mem_w
    assert w_in.shape[1] == c_gate + n_branch * d and n_branch == 3

    hf = h.reshape(t, d)
    h_bf = hf.astype(BF16)
    assert lru_w == mem_w
    w_t = w_in.T
    w_xz = jnp.concatenate([w_t[d_inner:c_dt], w_t[:d_inner]], axis=0).astype(BF16)
    w_dt = jnp.pad(w_t[c_dt:c_lru], ((0, LANES - n_heads), (0, 0))).astype(BF16)
    w_lm = w_t[c_lru:c_gate].astype(BF16)
    w_g = w_t[c_gate:].astype(BF16)

    xz = _matmul(h_bf, w_xz, BF16, b_is_transposed=True, name="proj_xz")
    dt_pad = _matmul(h_bf, w_dt, F32, b_is_transposed=True, name="proj_dt")
    lm = _matmul(h_bf, w_lm, BF16, b_is_transposed=True, name="proj_lru_mem")
    gate_pre = _matmul(h_bf, w_g, BF16, b_is_transposed=True, name="proj_gates")

    y_ssd = _ssd_branch(xz, dt_pad, ssd_conv_w, ssd_conv_b, ssd_dt_bias,
                        ssd_a_log, ssd_d, ssd_norm_w, batch=bsz, seq=seq)
    y_lru = _rglru_branch(lm, lru_conv_w, lru_conv_b, lru_w_a, lru_b_a,
                          lru_w_i, lru_b_i, lru_lambda, batch=bsz, seq=seq)
    kv = _matmul(mem.reshape(bsz * n_mem, d).astype(BF16), mem_w_kv.astype(BF16), BF16, name="mem_kv")
    y_mem = _memory_xattn(lm, 2, kv, batch=bsz, seq=seq, n_mem=n_mem)

    merged = _merge(y_ssd, y_lru, y_mem, w_branch_ssd.astype(BF16), w_branch_lru.astype(BF16),
                    w_branch_mem.astype(BF16), gate_pre, b_gate.astype(F32))
    pre1 = _matmul(merged, w_out.astype(BF16), F32, tn=512, res=hf, res_scale=alpha, name="out_proj")
    h1, h1_bf = _layernorm(pre1, ln1_g, ln1_b, with_bf16=True)

    n_keys = peer_keys.shape[2]
    q = _matmul(h1_bf, peer_w_q.astype(BF16), BF16, name="peer_q")
    hi, lo, gate = _peer_route(q, peer_keys.astype(BF16))
    hi_t, lo_t, gate_t = hi.T, lo.T, gate.T
    w_t = _peer_act(h1_bf, peer_u.astype(BF16), hi_t, lo_t, gate_t)
    dense_w = _peer_scatter(hi_t, lo_t, w_t, n_keys)
    pre2 = _matmul(dense_w, peer_v.astype(BF16), F32, tm=512, tn=256, res=h1, res_scale=alpha, name="peer_v")
    (out,) = _layernorm(pre2, ln2_g, ln2_b, with_bf16=False)
    return out.reshape(bsz, seq, d)


def kernel(x, mem, w_in, b_gate, ssd_conv_w, ssd_conv_b, ssd_dt_bias, ssd_a_log, ssd_d, ssd_norm_w, lru_conv_w, lru_conv_b, lru_w_a, lru_b_a, lru_w_i, lru_b_i, lru_lambda, mem_w_kv, w_branch_ssd, w_branch_lru, w_branch_mem, w_out, ln1_g, ln1_b, peer_w_q, peer_keys, peer_u, peer_v, ln2_g, ln2_b):
    params = (w_in, b_gate, ssd_conv_w, ssd_conv_b, ssd_dt_bias, ssd_a_log, ssd_d, ssd_norm_w, lru_conv_w,
              lru_conv_b, lru_w_a, lru_b_a, lru_w_i, lru_b_i, lru_lambda, mem_w_kv, w_branch_ssd, w_branch_lru,
              w_branch_mem, w_out, ln1_g, ln1_b, peer_w_q, peer_keys, peer_u, peer_v, ln2_g, ln2_b)
    depth = w_in.shape[0]
    h = x
    for l in range(depth):
        h = _layer(h, mem, *(p[l] for p in params), depth=depth)
    return h
```

```python
import functools
import math

import jax
import jax.numpy as jnp
from jax import lax
from jax.experimental import pallas as pl
from jax.experimental.pallas import tpu as pltpu

F32 = jnp.float32
BF16 = jnp.bfloat16
I32 = jnp.int32

SSD_HEAD_DIM = 64
SSD_D_STATE = 128
SSD_CONV = 4
LRU_CONV = 4
LRU_C = 8.0
MEM_HEADS = 4
PEER_TOPK = 16
LN_EPS = 1e-5
RMS_EPS = 1e-5

LANES = 128
SUBLANES = 8
VMEM_LIMIT_BYTES = 56 * 1024 * 1024

HIGHEST = lax.Precision.HIGHEST
NEG_INF = float("-inf")
ID_SENTINEL = 1e9


def _params(*semantics):
    return pltpu.CompilerParams(dimension_semantics=semantics, vmem_limit_bytes=VMEM_LIMIT_BYTES)


def _tile(dim, want, align=LANES):
    if dim <= want:
        return dim
    t = want - want % align
    while t > align and dim % t:
        t -= align
    assert dim % t == 0, (dim, want)
    return t


def _sigmoid(v):
    return 1.0 / (1.0 + jnp.exp(-v))


def _softplus(v):
    return jnp.maximum(v, 0.0) + jnp.log1p(jnp.exp(-jnp.abs(v)))


def _split3(v):
    hi = v.astype(BF16)
    rest = v - hi.astype(F32)
    mid = rest.astype(BF16)
    lo = (rest - mid.astype(F32)).astype(BF16)
    return hi, mid, lo


def _matmul_kernel(a_ref, b_ref, *rest, res_scale, b_is_transposed):
    o_ref = rest[-1]
    contract_b = 1 if b_is_transposed else 0
    acc = lax.dot_general(a_ref[...], b_ref[...], (((1,), (contract_b,)), ((), ())), preferred_element_type=F32)
    if len(rest) == 2:
        acc = acc + res_scale * rest[0][...].astype(F32)
    o_ref[...] = acc.astype(o_ref.dtype)


def _matmul(a, b, out_dtype, *, tm=1024, tn=1024, b_is_transposed=False, res=None, res_scale=1.0, name="matmul"):
    m, kdim = a.shape
    n = b.shape[0] if b_is_transposed else b.shape[1]
    tm, tn = _tile(m, tm), _tile(n, tn)
    b_spec = (pl.BlockSpec((tn, kdim), lambda i, j: (j, 0)) if b_is_transposed
              else pl.BlockSpec((kdim, tn), lambda i, j: (0, j)))
    in_specs = [pl.BlockSpec((tm, kdim), lambda i, j: (i, 0)), b_spec]
    args = [a, b]
    if res is not None:
        in_specs.append(pl.BlockSpec((tm, tn), lambda i, j: (i, j)))
        args.append(res)
    return pl.pallas_call(
        functools.partial(_matmul_kernel, res_scale=res_scale, b_is_transposed=b_is_transposed),
        grid=(m // tm, n // tn),
        in_specs=in_specs,
        out_specs=pl.BlockSpec((tm, tn), lambda i, j: (i, j)),
        out_shape=jax.ShapeDtypeStruct((m, n), out_dtype),
        compiler_params=_params("parallel", "parallel"),
        name=name,
    )(*args)


def _causal_conv(x_ref, hist_ref, w_ref, b_ref, first, n_taps):
    rows = x_ref.shape[0]
    x = x_ref[...].astype(F32)

    @pl.when(first)
    def _():
        hist_ref[...] = jnp.zeros(hist_ref.shape, F32)

    hist = hist_ref[...]
    row8 = lax.broadcasted_iota(I32, hist.shape, 0)
    w = w_ref[...]
    y = b_ref[...] + w[n_taps - 1:n_taps, :] * x
    for back in range(1, n_taps):
        tap = n_taps - 1 - back
        shifted = pltpu.roll(x, back, 0)
        head = jnp.where(row8 < back, pltpu.roll(hist, back, 0), shifted[0:SUBLANES, :])
        shifted = jnp.concatenate([head, shifted[SUBLANES:, :]], axis=0)
        y = y + w[tap:tap + 1, :] * shifted
    hist_ref[...] = x[rows - SUBLANES:rows, :]
    return y


def _ssd_kernel(z_ref, xbc_ref, dt_ref, cw_ref, cb_ref, dtb_ref, alog_ref, dskip_ref, nw_ref, e_ref,
                y_ref, xpad_ref, state_ref, *, n_groups, heads_per_group):
    rows = z_ref.shape[0]
    d_inner = z_ref.shape[1]
    gn = n_groups * SSD_D_STATE
    gw = heads_per_group * SSD_HEAD_DIM
    first = pl.program_id(1) == 0

    @pl.when(first)
    def _():
        state_ref[...] = jnp.zeros(state_ref.shape, F32)

    conv = _causal_conv(xbc_ref, xpad_ref, cw_ref, cb_ref, first, SSD_CONV)
    act = conv * _sigmoid(conv)

    dtv = _softplus(dt_ref[...] + dtb_ref[...])
    da = dtv * (-jnp.exp(alog_ref[...]))
    r_i = lax.broadcasted_iota(I32, (rows, rows), 0)
    c_i = lax.broadcasted_iota(I32, (rows, rows), 1)
    causal = r_i >= c_i
    cs = jnp.dot(causal.astype(F32), da, precision=HIGHEST, preferred_element_type=F32)
    cs_end = cs[rows - 1:rows, :]
    cs_t = cs.T

    stack = jnp.concatenate([dtv, jnp.exp(cs_end - cs), jnp.exp(cs)], axis=0)
    ex = jnp.dot(jnp.concatenate(_split3(stack), axis=1), e_ref[...], preferred_element_type=F32)
    dt_x, dte_x, ecs_x = ex[0:rows], ex[rows:2 * rows], ex[2 * rows:3 * rows]

    xs = act[:, 0:d_inner]
    xdt = xs * dt_x
    xdt_end = xdt * dte_x
    lane = lax.broadcasted_iota(I32, (1, gw), 1)

    for g in range(n_groups):
        lo, hi = g * gw, (g + 1) * gw
        b_f = act[:, d_inner + g * SSD_D_STATE:d_inner + (g + 1) * SSD_D_STATE]
        c_g = act[:, d_inner + gn + g * SSD_D_STATE:d_inner + gn + (g + 1) * SSD_D_STATE].astype(BF16)
        b_g = b_f.astype(BF16)
        cb = lax.dot_general(c_g, b_g, (((1,), (1,)), ((), ())), preferred_element_type=F32)
        st = state_ref[g]
        y_off = jnp.dot(c_g, st.astype(BF16), preferred_element_type=F32) * ecs_x[:, lo:hi]
        xdt_g = xdt[:, lo:hi]
        y_diag = jnp.zeros((rows, gw), F32)
        for r in range(heads_per_group):
            h = g * heads_per_group + r
            seg = cs[:, h:h + 1] - cs_t[h:h + 1, :]
            decay = jnp.where(causal, jnp.exp(jnp.where(causal, seg, 0.0)), 0.0)
            m_h = (cb * decay).astype(BF16)
            in_head = (lane >= r * SSD_HEAD_DIM) & (lane < (r + 1) * SSD_HEAD_DIM)
            x_h = jnp.where(in_head, xdt_g, 0.0).astype(BF16)
            y_diag = y_diag + jnp.dot(m_h, x_h, preferred_element_type=F32)
        new_st = st * ecs_x[rows - 1:rows, lo:hi] + jnp.dot(
            b_f.T.astype(BF16), xdt_end[:, lo:hi].astype(BF16), preferred_element_type=F32)
        state_ref[g] = new_st

        y = y_diag + y_off + xs[:, lo:hi] * dskip_ref[:, lo:hi]
        zg = z_ref[:, lo:hi].astype(F32)
        y = y * (zg * _sigmoid(zg))
        ms = jnp.mean(y * y, axis=-1, keepdims=True)
        y = y * lax.rsqrt(ms + RMS_EPS) * nw_ref[:, lo:hi]
        y_ref[:, lo:hi] = y.astype(y_ref.dtype)


def _ssd_branch(xz, dt_pad, conv_w, conv_b, dt_bias, a_log, d_skip, norm_w, *, batch, seq, chunk=128):
    t = xz.shape[0]
    xbc_w = conv_w.shape[1]
    d_inner = xz.shape[1] - xbc_w
    assert xbc_w % d_inner == 0
    n_heads = dt_bias.shape[0]
    n_groups = (xbc_w - d_inner) // (2 * SSD_D_STATE)
    hpg = n_heads // n_groups
    assert d_inner == n_heads * SSD_HEAD_DIM and d_inner // n_groups == hpg * SSD_HEAD_DIM
    assert n_heads <= LANES
    chunk = _tile(seq, chunk)
    nc = seq // chunk
    pad = LANES - n_heads
    dtb = jnp.pad(dt_bias.astype(F32), (0, pad)).reshape(1, LANES)
    alog = jnp.pad(a_log.astype(F32), (0, pad)).reshape(1, LANES)
    dskip = jnp.repeat(d_skip.astype(F32), SSD_HEAD_DIM).reshape(1, d_inner)
    expand = (jnp.arange(LANES)[:, None] == (jnp.arange(d_inner)[None, :] // SSD_HEAD_DIM)).astype(BF16)
    expand = jnp.concatenate([expand] * 3, axis=0)
    row = lambda b, c: (b * nc + c, 0)
    fixed = lambda b, c: (0, 0)
    return pl.pallas_call(
        functools.partial(_ssd_kernel, n_groups=n_groups, heads_per_group=hpg),
        grid=(batch, nc),
        in_specs=[pl.BlockSpec((chunk, d_inner), lambda b, c: (b * nc + c, xbc_w // d_inner)),
                  pl.BlockSpec((chunk, xbc_w), row),
                  pl.BlockSpec((chunk, LANES), row),
                  pl.BlockSpec((SSD_CONV, xbc_w), fixed),
                  pl.BlockSpec((1, xbc_w), fixed),
                  pl.BlockSpec((1, LANES), fixed),
                  pl.BlockSpec((1, LANES), fixed),
                  pl.BlockSpec((1, d_inner), fixed),
                  pl.BlockSpec((1, d_inner), fixed),
                  pl.BlockSpec((3 * LANES, d_inner), fixed)],
        out_specs=pl.BlockSpec((chunk, d_inner), row),
        out_shape=jax.ShapeDtypeStruct((t, d_inner), BF16),
        scratch_shapes=[pltpu.VMEM((SUBLANES, xbc_w), F32),
                        pltpu.VMEM((n_groups, SSD_D_STATE, hpg * SSD_HEAD_DIM), F32)],
        compiler_params=_params("arbitrary", "arbitrary"),
        name="ssd_scan",
    )(xz, xz, dt_pad, conv_w.astype(F32), conv_b.astype(F32).reshape(1, xbc_w), dtb, alog, dskip,
      norm_w.astype(F32).reshape(1, d_inner), expand)


def _rglru_kernel(g_ref, x_ref, cw_ref, cb_ref, wa_ref, ba_ref, wi_ref, bi_ref, lam_ref,
                  y_ref, xpad_ref, carry_ref):
    rows, width = x_ref.shape
    n_blk = wa_ref.shape[0]
    blk = wa_ref.shape[1]
    first = pl.program_id(2) == 0

    @pl.when(first)
    def _():
        carry_ref[...] = jnp.zeros(carry_ref.shape, F32)

    xc = _causal_conv(x_ref, xpad_ref, cw_ref, cb_ref, first, LRU_CONV)
    xcb = xc.astype(BF16)
    ra = jnp.concatenate([jnp.dot(xcb[:, k * blk:(k + 1) * blk], wa_ref[k], preferred_element_type=F32)
                          for k in range(n_blk)], axis=1)
    ri = jnp.concatenate([jnp.dot(xcb[:, k * blk:(k + 1) * blk], wi_ref[k], preferred_element_type=F32)
                          for k in range(n_blk)], axis=1)
    r = _sigmoid(ra + ba_ref[...])
    i = _sigmoid(ri + bi_ref[...])
    log_a = (-LRU_C) * r * _softplus(-lam_ref[...])
    a = jnp.exp(log_a)
    q = -jnp.tanh(log_a) * (a * a + 1.0)
    u = jnp.where(q > 0.0, q * lax.rsqrt(q), 0.0) * (i * xc)

    row_in_tile = lax.broadcasted_iota(I32, (rows, width), 0) & (SUBLANES - 1)
    shift = 1
    while shift < SUBLANES:
        keep = row_in_tile >= shift
        a_prev = jnp.where(keep, pltpu.roll(a, shift, 0), 1.0)
        u_prev = jnp.where(keep, pltpu.roll(u, shift, 0), 0.0)
        u = a * u_prev + u
        a = a * a_prev
        shift *= 2
    carry = carry_ref[0:1, :]
    tiles = []
    for g in range(rows // SUBLANES):
        rs = slice(g * SUBLANES, (g + 1) * SUBLANES)
        h_g = u[rs] + a[rs] * carry
        tiles.append(h_g)
        carry = h_g[SUBLANES - 1:SUBLANES, :]
    h = jnp.concatenate(tiles, axis=0)
    carry_ref[...] = jnp.broadcast_to(carry, carry_ref.shape)

    gv = g_ref[...].astype(F32)
    gelu = 0.5 * gv * (1.0 + jnp.tanh(math.sqrt(2.0 / math.pi) * (gv + 0.044715 * (gv * gv * gv))))
    y_ref[...] = (h * gelu).astype(y_ref.dtype)


def _rglru_branch(gx, conv_w, conv_b, w_a, b_a, w_i, b_i, lam, *, batch, seq, chunk=256, cw=512):
    t = gx.shape[0]
    width = conv_w.shape[1]
    n_blocks, blk, _ = w_a.shape
    cw = _tile(width, cw)
    assert cw % blk == 0
    bpc = cw // blk
    chunk = _tile(seq, chunk)
    nc = seq // chunk
    row = lambda b, j, c: (b * nc + c, j)
    col = lambda b, j, c: (0, j)
    wblk = lambda b, j, c: (j, 0, 0)
    vec = lambda v: v.astype(F32).reshape(1, width)
    return pl.pallas_call(
        _rglru_kernel,
        grid=(batch, width // cw, nc),
        in_specs=[pl.BlockSpec((chunk, cw), row),
                  pl.BlockSpec((chunk, cw), lambda b, j, c: (b * nc + c, width // cw + j)),
                  pl.BlockSpec((LRU_CONV, cw), col),
                  pl.BlockSpec((1, cw), col),
                  pl.BlockSpec((bpc, blk, blk), wblk),
                  pl.BlockSpec((1, cw), col),
                  pl.BlockSpec((bpc, blk, blk), wblk),
                  pl.BlockSpec((1, cw), col),
                  pl.BlockSpec((1, cw), col)],
        out_specs=pl.BlockSpec((chunk, cw), row),
        out_shape=jax.ShapeDtypeStruct((t, width), BF16),
        scratch_shapes=[pltpu.VMEM((SUBLANES, cw), F32), pltpu.VMEM((SUBLANES, cw), F32)],
        compiler_params=_params("arbitrary", "arbitrary", "arbitrary"),
        name="rglru_scan",
    )(gx, gx, conv_w.astype(F32), vec(conv_b), w_a.astype(BF16), vec(b_a), w_i.astype(BF16), vec(b_i),
      vec(lam))


def _xattn_kernel(q_ref, k_ref, v_ref, o_ref, *, n_heads):
    width = q_ref.shape[1]
    hd = width // n_heads
    scale = hd ** -0.5
    for h in range(n_heads):
        q = q_ref[:, h * hd:(h + 1) * hd]
        k = k_ref[:, h * hd:(h + 1) * hd]
        v = v_ref[:, h * hd:(h + 1) * hd]
        s = lax.dot_general(q, k, (((1,), (1,)), ((), ())), preferred_element_type=F32) * scale
        s = s - jnp.max(s, axis=-1, keepdims=True)
        p = jnp.exp(s)
        p = p / jnp.sum(p, axis=-1, keepdims=True)
        o = jnp.dot(p.astype(BF16), v, preferred_element_type=F32)
        o_ref[:, h * hd:(h + 1) * hd] = o.astype(o_ref.dtype)


def _memory_xattn(q_all, q_block, kv, *, batch, seq, n_mem, tq=512):
    t = q_all.shape[0]
    width = kv.shape[1] // 2
    tq = _tile(seq, tq)
    nq = seq // tq
    return pl.pallas_call(
        functools.partial(_xattn_kernel, n_heads=MEM_HEADS),
        grid=(batch, nq),
        in_specs=[pl.BlockSpec((tq, width), lambda b, i: (b * nq + i, q_block)),
                  pl.BlockSpec((n_mem, width), lambda b, i: (b, 0)),
                  pl.BlockSpec((n_mem, width), lambda b, i: (b, 1))],
        out_specs=pl.BlockSpec((tq, width), lambda b, i: (b * nq + i, 0)),
        out_shape=jax.ShapeDtypeStruct((t, width), BF16),
        compiler_params=_params("parallel", "parallel"),
        name="mem_xattn",
    )(q_all, kv, kv)


def _merge_kernel(ys_ref, yl_ref, ym_ref, ws_ref, wl_ref, wm_ref, g0_ref, g1_ref, g2_ref, bg_ref, o_ref):
    acc = None
    for k, (y_ref, w_ref, g_ref) in enumerate(((ys_ref, ws_ref, g0_ref), (yl_ref, wl_ref, g1_ref),
                                                (ym_ref, wm_ref, g2_ref))):
        gate = _sigmoid(g_ref[...].astype(F32) + bg_ref[k:k + 1, :])
        term = gate * jnp.dot(y_ref[...], w_ref[...], preferred_element_type=F32)
        acc = term if acc is None else acc + term
    o_ref[...] = acc.astype(o_ref.dtype)


def _merge(y_ssd, y_lru, y_mem, w_ssd, w_lru, w_mem, gate_pre, b_gate, *, tm=1024, tn=512):
    t, kdim = y_ssd.shape
    d = w_ssd.shape[1]
    tm, tn = _tile(t, tm), _tile(d, tn)
    nj = d // tn
    yspec = pl.BlockSpec((tm, kdim), lambda i, j: (i, 0))
    wspec = pl.BlockSpec((kdim, tn), lambda i, j: (0, j))
    gspec = lambda k: pl.BlockSpec((tm, tn), lambda i, j: (i, k * nj + j))
    return pl.pallas_call(
        _merge_kernel,
        grid=(t // tm, nj),
        in_specs=[yspec, yspec, yspec, wspec, wspec, wspec, gspec(0), gspec(1), gspec(2),
                  pl.BlockSpec((b_gate.shape[0], tn), lambda i, j: (0, j))],
        out_specs=pl.BlockSpec((tm, tn), lambda i, j: (i, j)),
        out_shape=jax.ShapeDtypeStruct((t, d), BF16),
        compiler_params=_params("parallel", "parallel"),
        name="branch_merge",
    )(y_ssd, y_lru, y_mem, w_ssd, w_lru, w_mem, gate_pre, gate_pre, gate_pre, b_gate)


def _layernorm_kernel(x_ref, g_ref, b_ref, o_ref, *rest):
    x = x_ref[...]
    mu = jnp.mean(x, axis=-1, keepdims=True)
    xc = x - mu
    var = jnp.mean(xc * xc, axis=-1, keepdims=True)
    y = xc * lax.rsqrt(var + LN_EPS) * g_ref[...] + b_ref[...]
    o_ref[...] = y
    if rest:
        rest[0][...] = y.astype(BF16)


def _layernorm(x, g, b, *, with_bf16, tm=256):
    t, d = x.shape
    tm = _tile(t, tm)
    spec = pl.BlockSpec((tm, d), lambda i: (i, 0))
    vspec = pl.BlockSpec((1, d), lambda i: (0, 0))
    out_shape = [jax.ShapeDtypeStruct((t, d), F32)]
    out_specs = [spec]
    if with_bf16:
        out_shape.append(jax.ShapeDtypeStruct((t, d), BF16))
        out_specs.append(spec)
    return pl.pallas_call(
        _layernorm_kernel,
        grid=(t // tm,),
        in_specs=[spec, vspec, vspec],
        out_specs=out_specs,
        out_shape=out_shape,
        compiler_params=_params("parallel"),
        name="layernorm",
    )(x, g.astype(F32).reshape(1, d), b.astype(F32).reshape(1, d))


def _extract_max(vals, ids):
    m = jnp.max(vals, axis=0, keepdims=True)
    sel = jnp.min(jnp.where(vals == m, ids, ID_SENTINEL), axis=0, keepdims=True)
    return m, sel, ids == sel


def _row_ids(shape, scale=1, offset=0):
    return ((lax.broadcasted_iota(I32, shape, 0) + offset) * scale).astype(F32)


def _route_scores(q, keys_ref):
    half = keys_ref.shape[3]
    return [lax.dot_general(keys_ref[0, side], q[:, side * half:(side + 1) * half], (((1,), (1,)), ((), ())),
                            preferred_element_type=F32) for side in range(2)]


def _route_select(scores, hi_ref, lo_ref, gate_ref, ts_ref, ti_ref, bs_ref):
    n_keys, tn = scores[0].shape
    k_top = PEER_TOPK
    key_id = _row_ids((n_keys, tn))

    for side in range(2):
        s = scores[side]
        for k in range(k_top):
            m, sel, hit = _extract_max(s, key_id)
            ts_ref[side, k:k + 1, :] = m
            ti_ref[side, k:k + 1, :] = sel
            s = jnp.where(hit, NEG_INF, s)

    s0, s1 = ts_ref[0], ts_ref[1]
    i0, i1 = ti_ref[0], ti_ref[1]
    pairs = [(i, j) for i in range(k_top) for j in range(k_top // (i + 1))]
    pairs += [None] * (-len(pairs) % SUBLANES)
    row8 = lax.broadcasted_iota(I32, (SUBLANES, tn), 0)

    def gather_rows(src, idx):
        if idx[0] % SUBLANES == 0 and all(idx[r] == idx[0] + r for r in range(SUBLANES)):
            return src[idx[0]:idx[0] + SUBLANES]
        out = jnp.broadcast_to(src[idx[0]:idx[0] + 1], (SUBLANES, tn))
        for r in range(1, SUBLANES):
            if idx[r] != idx[0]:
                out = jnp.where(row8 == r, src[idx[r]:idx[r] + 1], out)
        return out

    cand, a_id, b_id, pos = [], [], [], []
    for t0 in range(0, len(pairs), SUBLANES):
        tile = pairs[t0:t0 + SUBLANES]
        used = [p if p is not None else (0, 0) for p in tile]
        ii, jj = [p[0] for p in used], [p[1] for p in used]
        c = gather_rows(s0, ii) + gather_rows(s1, jj)
        n_live = sum(p is not None for p in tile)
        if n_live < SUBLANES:
            c = jnp.where(row8 < n_live, c, NEG_INF)
        cand.append(c)
        a_id.append(gather_rows(i0, ii))
        b_id.append(gather_rows(i1, jj))
        flat = [float(i * k_top + j) if p is not None else ID_SENTINEL for p, (i, j) in zip(tile, used)]
        pos_tile = jnp.full((SUBLANES, tn), flat[0], F32)
        for r in range(1, SUBLANES):
            pos_tile = jnp.where(row8 == r, flat[r], pos_tile)
        pos.append(pos_tile)
    cand = jnp.concatenate(cand, axis=0)
    a_id = jnp.concatenate(a_id, axis=0)
    b_id = jnp.concatenate(b_id, axis=0)
    pos = jnp.concatenate(pos, axis=0)

    for k in range(k_top):
        m, _, hit = _extract_max(cand, pos)
        bs_ref[k:k + 1, :] = m
        hi_ref[k:k + 1, :] = jnp.sum(jnp.where(hit, a_id, 0.0), axis=0, keepdims=True).astype(I32)
        lo_ref[k:k + 1, :] = jnp.sum(jnp.where(hit, b_id, 0.0), axis=0, keepdims=True).astype(I32)
        cand = jnp.where(hit, NEG_INF, cand)

    bs = bs_ref[...]
    e = jnp.exp(bs - bs[0:1])
    gate_ref[...] = e / jnp.sum(e, axis=0, keepdims=True)


def _peer_route_kernel(q_ref, keys_ref, hi_ref, lo_ref, gate_ref, ts_ref, ti_ref, bs_ref):
    _route_select(_route_scores(q_ref[...], keys_ref), hi_ref, lo_ref, gate_ref, ts_ref, ti_ref, bs_ref)


def _peer_route(q, keys, *, tn=1024):
    t = q.shape[0]
    n_heads, _, n_keys, half = keys.shape
    assert n_keys == LANES and PEER_TOPK % SUBLANES == 0
    tn = _tile(t, tn)
    slots = n_heads * PEER_TOPK
    ospec = pl.BlockSpec((PEER_TOPK, tn), lambda i, h: (h, i))
    return pl.pallas_call(
        _peer_route_kernel,
        grid=(t // tn, n_heads),
        in_specs=[pl.BlockSpec((tn, 2 * half), lambda i, h: (i, h)),
                  pl.BlockSpec((1, 2, n_keys, half), lambda i, h: (h, 0, 0, 0))],
        out_specs=[ospec, ospec, ospec],
        out_shape=[jax.ShapeDtypeStruct((slots, t), I32), jax.ShapeDtypeStruct((slots, t), I32),
                   jax.ShapeDtypeStruct((slots, t), F32)],
        scratch_shapes=[pltpu.VMEM((2, PEER_TOPK, tn), F32), pltpu.VMEM((2, PEER_TOPK, tn), F32),
                        pltpu.VMEM((PEER_TOPK, tn), F32)],
        compiler_params=_params("parallel", "parallel"),
        name="peer_route",
    )(q, keys)


def _erf_gelu(v):
    return 0.5 * v * (1.0 + lax.erf(v * (2.0 ** -0.5)))


def _pick_routed(dense, first_hi, hi, lo, acc):
    for c in range(dense.shape[1] // LANES):
        picked = jnp.take_along_axis(dense[:, c * LANES:(c + 1) * LANES], lo, axis=1)
        acc = jnp.where(hi == first_hi + c, picked, acc)
    return acc


def _peer_act_kernel(h_ref, u_ref, hi_ref, lo_ref, gate_ref, w_ref, acc_ref, dense_ref, *, n_chunks):
    i = pl.program_id(0)
    j = pl.program_id(1)
    per_chunk = dense_ref.shape[1] // LANES

    @pl.when((i == 0) & (j == 0))
    def _():
        dense_ref[...] = jnp.zeros(dense_ref.shape, F32)

    @pl.when(j == 0)
    def _():
        acc_ref[...] = jnp.zeros(acc_ref.shape, F32)

    hi = hi_ref[...]
    lo = lo_ref[...]
    acc_ref[...] = _pick_routed(dense_ref[...], (j - 1) * per_chunk, hi, lo, acc_ref[...])
    dense_ref[...] = lax.dot_general(h_ref[...], u_ref[...], (((1,), (1,)), ((), ())),
                                     preferred_element_type=F32)

    @pl.when(j == n_chunks - 1)
    def _():
        acc = _pick_routed(dense_ref[...], j * per_chunk, hi, lo, acc_ref[...])
        w_ref[...] = gate_ref[...] * _erf_gelu(acc)


def _peer_act(h_bf, u_tab, hi_t, lo_t, gate_t, *, tm=1024, te=1024):
    t, d = h_bf.shape
    n_exp = u_tab.shape[0]
    slots = hi_t.shape[1]
    assert slots == LANES
    tm, te = _tile(t, tm), _tile(n_exp, te)
    n_chunks = n_exp // te
    sspec = pl.BlockSpec((tm, slots), lambda i, j: (i, 0))
    return pl.pallas_call(
        functools.partial(_peer_act_kernel, n_chunks=n_chunks),
        grid=(t // tm, n_chunks),
        in_specs=[pl.BlockSpec((tm, d), lambda i, j: (i, 0)),
                  pl.BlockSpec((te, d), lambda i, j: (j, 0)),
                  sspec, sspec, sspec],
        out_specs=sspec,
        out_shape=jax.ShapeDtypeStruct((t, slots), F32),
        scratch_shapes=[pltpu.VMEM((tm, slots), F32), pltpu.VMEM((tm, te), F32)],
        compiler_params=_params("arbitrary", "arbitrary"),
        name="peer_act",
    )(h_bf, u_tab, hi_t, lo_t, gate_t)


def _peer_scatter_kernel(hi_ref, lo_ref, w_ref, o_ref, w3_ref, *, n_keys, pitch):
    tb, slots = hi_ref.shape
    row = lax.broadcasted_iota(I32, (tb, n_keys, slots), 1)
    hi = hi_ref[...][:, None, :]
    lo = lo_ref[...][:, None, :]
    wv = w_ref[...][:, None, :]
    a_t = jnp.where(row == hi, wv, 0.0).astype(BF16)
    r_t = jnp.where(row == lo, 1.0, 0.0).astype(BF16)
    w3 = lax.dot_general(a_t, r_t, (((2,), (2,)), ((0,), (0,))), preferred_element_type=F32)
    for t in range(tb):
        w3_ref[t * pitch:t * pitch + n_keys, :] = w3[t]
    for k in range(n_keys):
        o_ref[:, k * n_keys:(k + 1) * n_keys] = w3_ref[pl.ds(k, tb, stride=pitch), :].astype(o_ref.dtype)


def _peer_scatter(hi_t, lo_t, w_t, n_keys, *, tb=128):
    t, slots = hi_t.shape
    assert n_keys == LANES
    tb = _tile(t, tb, align=SUBLANES)
    pitch = n_keys + SUBLANES
    sspec = pl.BlockSpec((tb, slots), lambda i: (i, 0))
    return pl.pallas_call(
        functools.partial(_peer_scatter_kernel, n_keys=n_keys, pitch=pitch),
        grid=(t // tb,),
        in_specs=[sspec, sspec, sspec],
        out_specs=pl.BlockSpec((tb, n_keys * n_keys), lambda i: (i, 0)),
        out_shape=jax.ShapeDtypeStruct((t, n_keys * n_keys), BF16),
        scratch_shapes=[pltpu.VMEM((tb * pitch, n_keys), F32)],
        compiler_params=_params("parallel"),
        name="peer_scatter",
    )(hi_t, lo_t, w_t)


def _layer(h, mem, w_in, b_gate, ssd_conv_w, ssd_conv_b, ssd_dt_bias, ssd_a_log, ssd_d, ssd_norm_w,
           lru_conv_w, lru_conv_b, lru_w_a, lru_b_a, lru_w_i, lru_b_i, lru_lambda,
           mem_w_kv, w_branch_ssd, w_branch_lru, w_branch_mem, w_out, ln1_g, ln1_b,
           peer_w_q, peer_keys, peer_u, peer_v, ln2_g, ln2_b, *, depth):
    bsz, seq, d = h.shape
    t = bsz * seq
    n_mem = mem.shape[1]
    alpha = (2.0 * depth) ** 0.25

    d_inner = w_branch_ssd.shape[0]
    xbc_w = ssd_conv_w.shape[1]
    n_heads = ssd_dt_bias.shape[0]
    lru_w = w_branch_lru.shape[0]
    mem_w = w_branch_mem.shape[0]
    n_branch = b_gate.shape[0]
    c_dt = d_inner + xbc_w
    c_lru = c_dt + n_heads
    c_gate = c_lru + 2 * lru_w + mem_w
    assert w_in.shape[1] == c_gate + n_branch * d and n_branch == 3

    hf = h.reshape(t, d)
    h_bf = hf.astype(BF16)
    assert lru_w == mem_w
    w_t = w_in.T
    w_xz = jnp.concatenate([w_t[d_inner:c_dt], w_t[:d_inner]], axis=0).astype(BF16)
    w_dt = jnp.pad(w_t[c_dt:c_lru], ((0, LANES - n_heads), (0, 0))).astype(BF16)
    w_lm = w_t[c_lru:c_gate].astype(BF16)
    w_g = w_t[c_gate:].astype(BF16)

    xz = _matmul(h_bf, w_xz, BF16, b_is_transposed=True, name="proj_xz")
    dt_pad = _matmul(h_bf, w_dt, F32, b_is_transposed=True, name="proj_dt")
    lm = _matmul(h_bf, w_lm, BF16, b_is_transposed=True, name="proj_lru_mem")
    gate_pre = _matmul(h_bf, w_g, BF16, b_is_transposed=True, name="proj_gates")

    y_ssd = _ssd_branch(xz, dt_pad, ssd_conv_w, ssd_conv_b, ssd_dt_bias,
                        ssd_a_log, ssd_d, ssd_norm_w, batch=bsz, seq=seq)
    y_lru = _rglru_branch(lm, lru_conv_w, lru_conv_b, lru_w_a, lru_b_a,
                          lru_w_i, lru_b_i, lru_lambda, batch=bsz, seq=seq)
    kv = _matmul(mem.reshape(bsz * n_mem, d).astype(BF16), mem_w_kv.astype(BF16), BF16, name="mem_kv")
    y_mem = _memory_xattn(lm, 2, kv, batch=bsz, seq=seq, n_mem=n_mem)

    merged = _merge(y_ssd, y_lru, y_mem, w_branch_ssd.astype(BF16), w_branch_lru.astype(BF16),
                    w_branch_mem.astype(BF16), gate_pre, b_gate.astype(F32))
    pre1 = _matmul(merged, w_out.astype(BF16), F32, tn=512, res=hf, res_scale=alpha, name="out_proj")
    h1, h1_bf = _layernorm(pre1, ln1_g, ln1_b, with_bf16=True)

    n_keys = peer_keys.shape[2]
    q = _matmul(h1_bf, peer_w_q.astype(BF16), BF16, name="peer_q")
    hi, lo, gate = _peer_route(q, peer_keys.astype(BF16))
    hi_t, lo_t, gate_t = hi.T, lo.T, gate.T
    w_t = _peer_act(h1_bf, peer_u.astype(BF16), hi_t, lo_t, gate_t)
    dense_w = _peer_scatter(hi_t, lo_t, w_t, n_keys)
    pre2 = _matmul(dense_w, peer_v.astype(BF16), F32, tm=512, tn=256, res=h1, res_scale=alpha, name="peer_v")
    (out,) = _layernorm(pre2, ln2_g, ln2_b, with_bf16=False)
    return out.reshape(bsz, seq, d)


def kernel(x, mem, w_in, b_gate, ssd_conv_w, ssd_conv_b, ssd_dt_bias, ssd_a_log, ssd_d, ssd_norm_w, lru_conv_w, lru_conv_b, lru_w_a, lru_b_a, lru_w_i, lru_b_i, lru_lambda, mem_w_kv, w_branch_ssd, w_branch_lru, w_branch_mem, w_out, ln1_g, ln1_b, peer_w_q, peer_keys, peer_u, peer_v, ln2_g, ln2_b):
    params = (w_in, b_gate, ssd_conv_w, ssd_conv_b, ssd_dt_bias, ssd_a_log, ssd_d, ssd_norm_w, lru_conv_w,
              lru_conv_b, lru_w_a, lru_b_a, lru_w_i, lru_b_i, lru_lambda, mem_w_kv, w_branch_ssd, w_branch_lru,
              w_branch_mem, w_out, ln1_g, ln1_b, peer_w_q, peer_keys, peer_u, peer_v, ln2_g, ln2_b)
    depth = w_in.shape[0]
    h = x
    for l in range(depth):
        h = _layer(h, mem, *(p[l] for p in params), depth=depth)
    return h
```

```python
import functools
import math

import jax
import jax.numpy as jnp
from jax import lax
from jax.experimental import pallas as pl
from jax.experimental.pallas import tpu as pltpu

F32 = jnp.float32
BF16 = jnp.bfloat16
I32 = jnp.int32

SSD_HEAD_DIM = 64
SSD_D_STATE = 128
SSD_CONV = 4
LRU_CONV = 4
LRU_C = 8.0
MEM_HEADS = 4
PEER_TOPK = 16
LN_EPS = 1e-5
RMS_EPS = 1e-5

LANES = 128
SUBLANES = 8
VMEM_LIMIT_BYTES = 56 * 1024 * 1024

HIGHEST = lax.Precision.HIGHEST
NEG_INF = float("-inf")
ID_SENTINEL = 1e9


def _params(*semantics):
    return pltpu.CompilerParams(dimension_semantics=semantics, vmem_limit_bytes=VMEM_LIMIT_BYTES)


def _tile(dim, want, align=LANES):
    if dim <= want:
        return dim
    t = want - want % align
    while t > align and dim % t:
        t -= align
    assert dim % t == 0, (dim, want)
    return t


def _sigmoid(v):
    return 1.0 / (1.0 + jnp.exp(-v))


def _softplus(v):
    return jnp.maximum(v, 0.0) + jnp.log1p(jnp.exp(-jnp.abs(v)))


def _split3(v):
    hi = v.astype(BF16)
    rest = v - hi.astype(F32)
    mid = rest.astype(BF16)
    lo = (rest - mid.astype(F32)).astype(BF16)
    return hi, mid, lo


def _matmul_kernel(a_ref, b_ref, *rest, res_scale, b_is_transposed):
    o_ref = rest[-1]
    contract_b = 1 if b_is_transposed else 0
    acc = lax.dot_general(a_ref[...], b_ref[...], (((1,), (contract_b,)), ((), ())), preferred_element_type=F32)
    if len(rest) == 2:
        acc = acc + res_scale * rest[0][...].astype(F32)
    o_ref[...] = acc.astype(o_ref.dtype)


def _matmul(a, b, out_dtype, *, tm=1024, tn=1024, b_is_transposed=False, res=None, res_scale=1.0, name="matmul"):
    m, kdim = a.shape
    n = b.shape[0] if b_is_transposed else b.shape[1]
    tm, tn = _tile(m, tm), _tile(n, tn)
    b_spec = (pl.BlockSpec((tn, kdim), lambda i, j: (j, 0)) if b_is_transposed
              else pl.BlockSpec((kdim, tn), lambda i, j: (0, j)))
    in_specs = [pl.BlockSpec((tm, kdim), lambda i, j: (i, 0)), b_spec]
    args = [a, b]
    if res is not None:
        in_specs.append(pl.BlockSpec((tm, tn), lambda i, j: (i, j)))
        args.append(res)
    return pl.pallas_call(
        functools.partial(_matmul_kernel, res_scale=res_scale, b_is_transposed=b_is_transposed),
        grid=(m // tm, n // tn),
        in_specs=in_specs,
        out_specs=pl.BlockSpec((tm, tn), lambda i, j: (i, j)),
        out_shape=jax.ShapeDtypeStruct((m, n), out_dtype),
        compiler_params=_params("parallel", "parallel"),
        name=name,
    )(*args)


def _causal_conv(x_ref, hist_ref, w_ref, b_ref, first, n_taps):
    rows = x_ref.shape[0]
    x = x_ref[...].astype(F32)

    @pl.when(first)
    def _():
        hist_ref[...] = jnp.zeros(hist_ref.shape, F32)

    hist = hist_ref[...]
    row8 = lax.broadcasted_iota(I32, hist.shape, 0)
    w = w_ref[...]
    y = b_ref[...] + w[n_taps - 1:n_taps, :] * x
    for back in range(1, n_taps):
        tap = n_taps - 1 - back
        shifted = pltpu.roll(x, back, 0)
        head = jnp.where(row8 < back, pltpu.roll(hist, back, 0), shifted[0:SUBLANES, :])
        shifted = jnp.concatenate([head, shifted[SUBLANES:, :]], axis=0)
        y = y + w[tap:tap + 1, :] * shifted
    hist_ref[...] = x[rows - SUBLANES:rows, :]
    return y


def _ssd_kernel(z_ref, xbc_ref, dt_ref, cw_ref, cb_ref, dtb_ref, alog_ref, dskip_ref, nw_ref, e_ref,
                y_ref, hist_ref, state_ref, *, n_groups, heads_per_group):
    rows = z_ref.shape[0]
    d_inner = z_ref.shape[1]
    gn = n_groups * SSD_D_STATE
    gw = heads_per_group * SSD_HEAD_DIM
    first = pl.program_id(1) == 0

    @pl.when(first)
    def _():
        state_ref[...] = jnp.zeros(state_ref.shape, F32)

    conv = _causal_conv(xbc_ref, hist_ref, cw_ref, cb_ref, first, SSD_CONV)
    act = conv * _sigmoid(conv)

    dtv = _softplus(dt_ref[...] + dtb_ref[...])
    da = dtv * (-jnp.exp(alog_ref[...]))
    r_i = lax.broadcasted_iota(I32, (rows, rows), 0)
    c_i = lax.broadcasted_iota(I32, (rows, rows), 1)
    causal = r_i >= c_i
    cs = jnp.dot(causal.astype(F32), da, precision=HIGHEST, preferred_element_type=F32)
    cs_end = cs[rows - 1:rows, :]
    cs_t = cs.T

    stack = jnp.concatenate([dtv, jnp.exp(cs_end - cs), jnp.exp(cs)], axis=0)
    ex = jnp.dot(jnp.concatenate(_split3(stack), axis=1), e_ref[...], preferred_element_type=F32)
    dt_x, dte_x, ecs_x = ex[0:rows], ex[rows:2 * rows], ex[2 * rows:3 * rows]

    xs = act[:, 0:d_inner]
    xdt = xs * dt_x
    xdt_end = xdt * dte_x
    lane = lax.broadcasted_iota(I32, (1, gw), 1)

    for g in range(n_groups):
        lo, hi = g * gw, (g + 1) * gw
        b_f = act[:, d_inner + g * SSD_D_STATE:d_inner + (g + 1) * SSD_D_STATE]
        c_g = act[:, d_inner + gn + g * SSD_D_STATE:d_inner + gn + (g + 1) * SSD_D_STATE].astype(BF16)
        b_g = b_f.astype(BF16)
        cb = lax.dot_general(c_g, b_g, (((1,), (1,)), ((), ())), preferred_element_type=F32)
        st = state_ref[g]
        y_off = jnp.dot(c_g, st.astype(BF16), preferred_element_type=F32) * ecs_x[:, lo:hi]
        xdt_g = xdt[:, lo:hi]
        y_diag = jnp.zeros((rows, gw), F32)
        for r in range(heads_per_group):
            h = g * heads_per_group + r
            seg = cs[:, h:h + 1] - cs_t[h:h + 1, :]
            decay = jnp.where(causal, jnp.exp(jnp.where(causal, seg, 0.0)), 0.0)
            m_h = (cb * decay).astype(BF16)
            in_head = (lane >= r * SSD_HEAD_DIM) & (lane < (r + 1) * SSD_HEAD_DIM)
            x_h = jnp.where(in_head, xdt_g, 0.0).astype(BF16)
            y_diag = y_diag + jnp.dot(m_h, x_h, preferred_element_type=F32)
        new_st = st * ecs_x[rows - 1:rows, lo:hi] + jnp.dot(
            b_f.T.astype(BF16), xdt_end[:, lo:hi].astype(BF16), preferred_element_type=F32)
        state_ref[g] = new_st

        y = y_diag + y_off + xs[:, lo:hi] * dskip_ref[:, lo:hi]
        zg = z_ref[:, lo:hi].astype(F32)
        y = y * (zg * _sigmoid(zg))
        ms = jnp.mean(y * y, axis=-1, keepdims=True)
        y = y * lax.rsqrt(ms + RMS_EPS) * nw_ref[:, lo:hi]
        y_ref[:, lo:hi] = y.astype(y_ref.dtype)


def _ssd_branch(xz, dt_pad, conv_w, conv_b, dt_bias, a_log, d_skip, norm_w, *, batch, seq, chunk=128):
    t = xz.shape[0]
    xbc_w = conv_w.shape[1]
    d_inner = xz.shape[1] - xbc_w
    assert xbc_w % d_inner == 0
    n_heads = dt_bias.shape[0]
    n_groups = (xbc_w - d_inner) // (2 * SSD_D_STATE)
    hpg = n_heads // n_groups
    assert d_inner == n_heads * SSD_HEAD_DIM and d_inner // n_groups == hpg * SSD_HEAD_DIM
    assert n_heads <= LANES
    chunk = _tile(seq, chunk)
    nc = seq // chunk
    pad = LANES - n_heads
    dtb = jnp.pad(dt_bias.astype(F32), (0, pad)).reshape(1, LANES)
    alog = jnp.pad(a_log.astype(F32), (0, pad)).reshape(1, LANES)
    dskip = jnp.repeat(d_skip.astype(F32), SSD_HEAD_DIM).reshape(1, d_inner)
    expand = (jnp.arange(LANES)[:, None] == (jnp.arange(d_inner)[None, :] // SSD_HEAD_DIM)).astype(BF16)
    expand = jnp.concatenate([expand] * 3, axis=0)
    row = lambda b, c: (b * nc + c, 0)
    fixed = lambda b, c: (0, 0)
    return pl.pallas_call(
        functools.partial(_ssd_kernel, n_groups=n_groups, heads_per_group=hpg),
        grid=(batch, nc),
        in_specs=[pl.BlockSpec((chunk, d_inner), lambda b, c: (b * nc + c, xbc_w // d_inner)),
                  pl.BlockSpec((chunk, xbc_w), row),
                  pl.BlockSpec((chunk, LANES), row),
                  pl.BlockSpec((SSD_CONV, xbc_w), fixed),
                  pl.BlockSpec((1, xbc_w), fixed),
                  pl.BlockSpec((1, LANES), fixed),
                  pl.BlockSpec((1, LANES), fixed),
                  pl.BlockSpec((1, d_inner), fixed),
                  pl.BlockSpec((1, d_inner), fixed),
                  pl.BlockSpec((3 * LANES, d_inner), fixed)],
        out_specs=pl.BlockSpec((chunk, d_inner), row),
        out_shape=jax.ShapeDtypeStruct((t, d_inner), BF16),
        scratch_shapes=[pltpu.VMEM((SUBLANES, xbc_w), F32),
                        pltpu.VMEM((n_groups, SSD_D_STATE, hpg * SSD_HEAD_DIM), F32)],
        compiler_params=_params("arbitrary", "arbitrary"),
        name="ssd_scan",
    )(xz, xz, dt_pad, conv_w.astype(F32), conv_b.astype(F32).reshape(1, xbc_w), dtb, alog, dskip,
      norm_w.astype(F32).reshape(1, d_inner), expand)


def _rglru_kernel(g_ref, x_ref, cw_ref, cb_ref, wa_ref, ba_ref, wi_ref, bi_ref, lam_ref,
                  y_ref, hist_ref, carry_ref):
    rows, width = x_ref.shape
    n_blk = wa_ref.shape[0]
    blk = wa_ref.shape[1]
    first = pl.program_id(2) == 0

    @pl.when(first)
    def _():
        carry_ref[...] = jnp.zeros(carry_ref.shape, F32)

    xc = _causal_conv(x_ref, hist_ref, cw_ref, cb_ref, first, LRU_CONV)
    xcb = xc.astype(BF16)
    ra = jnp.concatenate([jnp.dot(xcb[:, k * blk:(k + 1) * blk], wa_ref[k], preferred_element_type=F32)
                          for k in range(n_blk)], axis=1)
    ri = jnp.concatenate([jnp.dot(xcb[:, k * blk:(k + 1) * blk], wi_ref[k], preferred_element_type=F32)
                          for k in range(n_blk)], axis=1)
    r = _sigmoid(ra + ba_ref[...])
    i = _sigmoid(ri + bi_ref[...])
    log_a = (-LRU_C) * r * _softplus(-lam_ref[...])
    a = jnp.exp(log_a)
    q = -jnp.tanh(log_a) * (a * a + 1.0)
    u = jnp.where(q > 0.0, q * lax.rsqrt(q), 0.0) * (i * xc)

    row_in_tile = lax.broadcasted_iota(I32, (rows, width), 0) & (SUBLANES - 1)
    shift = 1
    while shift < SUBLANES:
        keep = row_in_tile >= shift
        a_prev = jnp.where(keep, pltpu.roll(a, shift, 0), 1.0)
        u_prev = jnp.where(keep, pltpu.roll(u, shift, 0), 0.0)
        u = a * u_prev + u
        a = a * a_prev
        shift *= 2
    carry = carry_ref[0:1, :]
    tiles = []
    for g in range(rows // SUBLANES):
        rs = slice(g * SUBLANES, (g + 1) * SUBLANES)
        h_g = u[rs] + a[rs] * carry
        tiles.append(h_g)
        carry = h_g[SUBLANES - 1:SUBLANES, :]
    h = jnp.concatenate(tiles, axis=0)
    carry_ref[...] = jnp.broadcast_to(carry, carry_ref.shape)

    gv = g_ref[...].astype(F32)
    gelu = 0.5 * gv * (1.0 + jnp.tanh(math.sqrt(2.0 / math.pi) * (gv + 0.044715 * (gv * gv * gv))))
    y_ref[...] = (h * gelu).astype(y_ref.dtype)


def _rglru_branch(gx, conv_w, conv_b, w_a, b_a, w_i, b_i, lam, *, batch, seq, chunk=256, cw=512):
    t = gx.shape[0]
    width = conv_w.shape[1]
    n_blocks, blk, _ = w_a.shape
    cw = _tile(width, cw)
    assert cw % blk == 0
    bpc = cw // blk
    chunk = _tile(seq, chunk)
    nc = seq // chunk
    row = lambda b, j, c: (b * nc + c, j)
    col = lambda b, j, c: (0, j)
    wblk = lambda b, j, c: (j, 0, 0)
    vec = lambda v: v.astype(F32).reshape(1, width)
    return pl.pallas_call(
        _rglru_kernel,
        grid=(batch, width // cw, nc),
        in_specs=[pl.BlockSpec((chunk, cw), row),
                  pl.BlockSpec((chunk, cw), lambda b, j, c: (b * nc + c, width // cw + j)),
                  pl.BlockSpec((LRU_CONV, cw), col),
                  pl.BlockSpec((1, cw), col),
                  pl.BlockSpec((bpc, blk, blk), wblk),
                  pl.BlockSpec((1, cw), col),
                  pl.BlockSpec((bpc, blk, blk), wblk),
                  pl.BlockSpec((1, cw), col),
                  pl.BlockSpec((1, cw), col)],
        out_specs=pl.BlockSpec((chunk, cw), row),
        out_shape=jax.ShapeDtypeStruct((t, width), BF16),
        scratch_shapes=[pltpu.VMEM((SUBLANES, cw), F32), pltpu.VMEM((SUBLANES, cw), F32)],
        compiler_params=_params("arbitrary", "arbitrary", "arbitrary"),
        name="rglru_scan",
    )(gx, gx, conv_w.astype(F32), vec(conv_b), w_a.astype(BF16), vec(b_a), w_i.astype(BF16), vec(b_i),
      vec(lam))


def _xattn_kernel(q_ref, k_ref, v_ref, o_ref, *, n_heads):
    width = q_ref.shape[1]
    hd = width // n_heads
    scale = hd ** -0.5
    for h in range(n_heads):
        q = q_ref[:, h * hd:(h + 1) * hd]
        k = k_ref[:, h * hd:(h + 1) * hd]
        v = v_ref[:, h * hd:(h + 1) * hd]
        s = lax.dot_general(q, k, (((1,), (1,)), ((), ())), preferred_element_type=F32) * scale
        s = s - jnp.max(s, axis=-1, keepdims=True)
        p = jnp.exp(s)
        p = p / jnp.sum(p, axis=-1, keepdims=True)
        o = jnp.dot(p.astype(BF16), v, preferred_element_type=F32)
        o_ref[:, h * hd:(h + 1) * hd] = o.astype(o_ref.dtype)


def _memory_xattn(q_all, q_block, kv, *, batch, seq, n_mem, tq=512):
    t = q_all.shape[0]
    width = kv.shape[1] // 2
    tq = _tile(seq, tq)
    nq = seq // tq
    return pl.pallas_call(
        functools.partial(_xattn_kernel, n_heads=MEM_HEADS),
        grid=(batch, nq),
        in_specs=[pl.BlockSpec((tq, width), lambda b, i: (b * nq + i, q_block)),
                  pl.BlockSpec((n_mem, width), lambda b, i: (b, 0)),
                  pl.BlockSpec((n_mem, width), lambda b, i: (b, 1))],
        out_specs=pl.BlockSpec((tq, width), lambda b, i: (b * nq + i, 0)),
        out_shape=jax.ShapeDtypeStruct((t, width), BF16),
        compiler_params=_params("parallel", "parallel"),
        name="mem_xattn",
    )(q_all, kv, kv)


def _merge_kernel(ys_ref, yl_ref, ym_ref, ws_ref, wl_ref, wm_ref, g0_ref, g1_ref, g2_ref, bg_ref, o_ref):
    acc = None
    for k, (y_ref, w_ref, g_ref) in enumerate(((ys_ref, ws_ref, g0_ref), (yl_ref, wl_ref, g1_ref),
                                                (ym_ref, wm_ref, g2_ref))):
        gate = _sigmoid(g_ref[...].astype(F32) + bg_ref[k:k + 1, :])
        term = gate * jnp.dot(y_ref[...], w_ref[...], preferred_element_type=F32)
        acc = term if acc is None else acc + term
    o_ref[...] = acc.astype(o_ref.dtype)


def _merge(y_ssd, y_lru, y_mem, w_ssd, w_lru, w_mem, gate_pre, b_gate, *, tm=1024, tn=512):
    t, kdim = y_ssd.shape
    d = w_ssd.shape[1]
    tm, tn = _tile(t, tm), _tile(d, tn)
    nj = d // tn
    yspec = pl.BlockSpec((tm, kdim), lambda i, j: (i, 0))
    wspec = pl.BlockSpec((kdim, tn), lambda i, j: (0, j))
    gspec = lambda k: pl.BlockSpec((tm, tn), lambda i, j: (i, k * nj + j))
    return pl.pallas_call(
        _merge_kernel,
        grid=(t // tm, nj),
        in_specs=[yspec, yspec, yspec, wspec, wspec, wspec, gspec(0), gspec(1), gspec(2),
                  pl.BlockSpec((b_gate.shape[0], tn), lambda i, j: (0, j))],
        out_specs=pl.BlockSpec((tm, tn), lambda i, j: (i, j)),
        out_shape=jax.ShapeDtypeStruct((t, d), BF16),
        compiler_params=_params("parallel", "parallel"),
        name="branch_merge",
    )(y_ssd, y_lru, y_mem, w_ssd, w_lru, w_mem, gate_pre, gate_pre, gate_pre, b_gate)


def _layernorm_kernel(x_ref, g_ref, b_ref, o_ref, *rest):
    x = x_ref[...]
    mu = jnp.mean(x, axis=-1, keepdims=True)
    xc = x - mu
    var = jnp.mean(xc * xc, axis=-1, keepdims=True)
    y = xc * lax.rsqrt(var + LN_EPS) * g_ref[...] + b_ref[...]
    o_ref[...] = y
    if rest:
        rest[0][...] = y.astype(BF16)


def _layernorm(x, g, b, *, with_bf16, tm=512):
    t, d = x.shape
    tm = _tile(t, tm)
    spec = pl.BlockSpec((tm, d), lambda i: (i, 0))
    vspec = pl.BlockSpec((1, d), lambda i: (0, 0))
    out_shape = [jax.ShapeDtypeStruct((t, d), F32)]
    out_specs = [spec]
    if with_bf16:
        out_shape.append(jax.ShapeDtypeStruct((t, d), BF16))
        out_specs.append(spec)
    return pl.pallas_call(
        _layernorm_kernel,
        grid=(t // tm,),
        in_specs=[spec, vspec, vspec],
        out_specs=out_specs,
        out_shape=out_shape,
        compiler_params=_params("parallel"),
        name="layernorm",
    )(x, g.astype(F32).reshape(1, d), b.astype(F32).reshape(1, d))


def _extract_max(vals, ids):
    m = jnp.max(vals, axis=0, keepdims=True)
    sel = jnp.min(jnp.where(vals == m, ids, ID_SENTINEL), axis=0, keepdims=True)
    return m, sel, ids == sel


def _row_ids(shape, scale=1, offset=0):
    return ((lax.broadcasted_iota(I32, shape, 0) + offset) * scale).astype(F32)


def _route_scores(q, keys_ref):
    half = keys_ref.shape[3]
    return [lax.dot_general(keys_ref[0, side], q[:, side * half:(side + 1) * half], (((1,), (1,)), ((), ())),
                            preferred_element_type=F32) for side in range(2)]


def _route_select(scores, hi_ref, lo_ref, gate_ref, ts_ref, ti_ref, bs_ref):
    n_keys, tn = scores[0].shape
    k_top = PEER_TOPK
    key_id = _row_ids((n_keys, tn))

    for side in range(2):
        s = scores[side]
        for k in range(k_top):
            m, sel, hit = _extract_max(s, key_id)
            ts_ref[side, k:k + 1, :] = m
            ti_ref[side, k:k + 1, :] = sel
            s = jnp.where(hit, NEG_INF, s)

    s0, s1 = ts_ref[0], ts_ref[1]
    i0, i1 = ti_ref[0], ti_ref[1]
    pairs = [(i, j) for i in range(k_top) for j in range(k_top // (i + 1))]
    pairs += [None] * (-len(pairs) % SUBLANES)
    row8 = lax.broadcasted_iota(I32, (SUBLANES, tn), 0)

    def gather_rows(src, idx):
        if idx[0] % SUBLANES == 0 and all(idx[r] == idx[0] + r for r in range(SUBLANES)):
            return src[idx[0]:idx[0] + SUBLANES]
        out = jnp.broadcast_to(src[idx[0]:idx[0] + 1], (SUBLANES, tn))
        for r in range(1, SUBLANES):
            if idx[r] != idx[0]:
                out = jnp.where(row8 == r, src[idx[r]:idx[r] + 1], out)
        return out

    cand, a_id, b_id, pos = [], [], [], []
    for t0 in range(0, len(pairs), SUBLANES):
        tile = pairs[t0:t0 + SUBLANES]
        used = [p if p is not None else (0, 0) for p in tile]
        ii, jj = [p[0] for p in used], [p[1] for p in used]
        c = gather_rows(s0, ii) + gather_rows(s1, jj)
        n_live = sum(p is not None for p in tile)
        if n_live < SUBLANES:
            c = jnp.where(row8 < n_live, c, NEG_INF)
        cand.append(c)
        a_id.append(gather_rows(i0, ii))
        b_id.append(gather_rows(i1, jj))
        flat = [float(i * k_top + j) if p is not None else ID_SENTINEL for p, (i, j) in zip(tile, used)]
        pos_tile = jnp.full((SUBLANES, tn), flat[0], F32)
        for r in range(1, SUBLANES):
            pos_tile = jnp.where(row8 == r, flat[r], pos_tile)
        pos.append(pos_tile)
    cand = jnp.concatenate(cand, axis=0)
    a_id = jnp.concatenate(a_id, axis=0)
    b_id = jnp.concatenate(b_id, axis=0)
    pos = jnp.concatenate(pos, axis=0)

    for k in range(k_top):
        m, _, hit = _extract_max(cand, pos)
        bs_ref[k:k + 1, :] = m
        hi_ref[k:k + 1, :] = jnp.sum(jnp.where(hit, a_id, 0.0), axis=0, keepdims=True).astype(I32)
        lo_ref[k:k + 1, :] = jnp.sum(jnp.where(hit, b_id, 0.0), axis=0, keepdims=True).astype(I32)
        cand = jnp.where(hit, NEG_INF, cand)

    bs = bs_ref[...]
    e = jnp.exp(bs - bs[0:1])
    gate_ref[...] = e / jnp.sum(e, axis=0, keepdims=True)


def _peer_route_kernel(q_ref, keys_ref, hi_ref, lo_ref, gate_ref, ts_ref, ti_ref, bs_ref):
    _route_select(_route_scores(q_ref[...], keys_ref), hi_ref, lo_ref, gate_ref, ts_ref, ti_ref, bs_ref)


def _peer_route(q, keys, *, tn=1024):
    t = q.shape[0]
    n_heads, _, n_keys, half = keys.shape
    assert n_keys == LANES and PEER_TOPK % SUBLANES == 0
    tn = _tile(t, tn)
    slots = n_heads * PEER_TOPK
    ospec = pl.BlockSpec((PEER_TOPK, tn), lambda i, h: (h, i))
    return pl.pallas_call(
        _peer_route_kernel,
        grid=(t // tn, n_heads),
        in_specs=[pl.BlockSpec((tn, 2 * half), lambda i, h: (i, h)),
                  pl.BlockSpec((1, 2, n_keys, half), lambda i, h: (h, 0, 0, 0))],
        out_specs=[ospec, ospec, ospec],
        out_shape=[jax.ShapeDtypeStruct((slots, t), I32), jax.ShapeDtypeStruct((slots, t), I32),
                   jax.ShapeDtypeStruct((slots, t), F32)],
        scratch_shapes=[pltpu.VMEM((2, PEER_TOPK, tn), F32), pltpu.VMEM((2, PEER_TOPK, tn), F32),
                        pltpu.VMEM((PEER_TOPK, tn), F32)],
        compiler_params=_params("parallel", "parallel"),
        name="peer_route",
    )(q, keys)


def _erf_gelu(v):
    return 0.5 * v * (1.0 + lax.erf(v * (2.0 ** -0.5)))


def _pick_routed(dense, first_hi, hi, lo, acc):
    for c in range(dense.shape[1] // LANES):
        picked = jnp.take_along_axis(dense[:, c * LANES:(c + 1) * LANES], lo, axis=1)
        acc = jnp.where(hi == first_hi + c, picked, acc)
    return acc


def _peer_act_kernel(h_ref, u_ref, hi_ref, lo_ref, gate_ref, w_ref, acc_ref, dense_ref, *, n_chunks):
    i = pl.program_id(0)
    j = pl.program_id(1)
    per_chunk = dense_ref.shape[1] // LANES

    @pl.when((i == 0) & (j == 0))
    def _():
        dense_ref[...] = jnp.zeros(dense_ref.shape, F32)

    @pl.when(j == 0)
    def _():
        acc_ref[...] = jnp.zeros(acc_ref.shape, F32)

    hi = hi_ref[...]
    lo = lo_ref[...]
    acc_ref[...] = _pick_routed(dense_ref[...], (j - 1) * per_chunk, hi, lo, acc_ref[...])
    dense_ref[...] = lax.dot_general(h_ref[...], u_ref[...], (((1,), (1,)), ((), ())),
                                     preferred_element_type=F32)

    @pl.when(j == n_chunks - 1)
    def _():
        acc = _pick_routed(dense_ref[...], j * per_chunk, hi, lo, acc_ref[...])
        w_ref[...] = gate_ref[...] * _erf_gelu(acc)


def _peer_act(h_bf, u_tab, hi_t, lo_t, gate_t, *, tm=1024, te=1024):
    t, d = h_bf.shape
    n_exp = u_tab.shape[0]
    slots = hi_t.shape[1]
    assert slots == LANES
    tm, te = _tile(t, tm), _tile(n_exp, te)
    n_chunks = n_exp // te
    sspec = pl.BlockSpec((tm, slots), lambda i, j: (i, 0))
    return pl.pallas_call(
        functools.partial(_peer_act_kernel, n_chunks=n_chunks),
        grid=(t // tm, n_chunks),
        in_specs=[pl.BlockSpec((tm, d), lambda i, j: (i, 0)),
                  pl.BlockSpec((te, d), lambda i, j: (j, 0)),
                  sspec, sspec, sspec],
        out_specs=sspec,
        out_shape=jax.ShapeDtypeStruct((t, slots), F32),
        scratch_shapes=[pltpu.VMEM((tm, slots), F32), pltpu.VMEM((tm, te), F32)],
        compiler_params=_params("arbitrary", "arbitrary"),
        name="peer_act",
    )(h_bf, u_tab, hi_t, lo_t, gate_t)


def _peer_scatter_kernel(hi_ref, lo_ref, w_ref, o_ref, w3_ref, *, n_keys, pitch):
    tb, slots = hi_ref.shape
    row = lax.broadcasted_iota(I32, (tb, n_keys, slots), 1)
    hi = hi_ref[...][:, None, :]
    lo = lo_ref[...][:, None, :]
    wv = w_ref[...][:, None, :]
    a_t = jnp.where(row == hi, wv, 0.0).astype(BF16)
    r_t = jnp.where(row == lo, 1.0, 0.0).astype(BF16)
    w3 = lax.dot_general(a_t, r_t, (((2,), (2,)), ((0,), (0,))), preferred_element_type=F32)
    for t in range(tb):
        w3_ref[t * pitch:t * pitch + n_keys, :] = w3[t]
    for k in range(n_keys):
        o_ref[:, k * n_keys:(k + 1) * n_keys] = w3_ref[pl.ds(k, tb, stride=pitch), :].astype(o_ref.dtype)


def _peer_scatter(hi_t, lo_t, w_t, n_keys, *, tb=128):
    t, slots = hi_t.shape
    assert n_keys == LANES
    tb = _tile(t, tb, align=SUBLANES)
    pitch = n_keys + SUBLANES
    sspec = pl.BlockSpec((tb, slots), lambda i: (i, 0))
    return pl.pallas_call(
        functools.partial(_peer_scatter_kernel, n_keys=n_keys, pitch=pitch),
        grid=(t // tb,),
        in_specs=[sspec, sspec, sspec],
        out_specs=pl.BlockSpec((tb, n_keys * n_keys), lambda i: (i, 0)),
        out_shape=jax.ShapeDtypeStruct((t, n_keys * n_keys), BF16),
        scratch_shapes=[pltpu.VMEM((tb * pitch, n_keys), F32)],
        compiler_params=_params("parallel"),
        name="peer_scatter",
    )(hi_t, lo_t, w_t)


def _layer(h, mem, w_in, b_gate, ssd_conv_w, ssd_conv_b, ssd_dt_bias, ssd_a_log, ssd_d, ssd_norm_w,
           lru_conv_w, lru_conv_b, lru_w_a, lru_b_a, lru_w_i, lru_b_i, lru_lambda,
           mem_w_kv, w_branch_ssd, w_branch_lru, w_branch_mem, w_out, ln1_g, ln1_b,
           peer_w_q, peer_keys, peer_u, peer_v, ln2_g, ln2_b, *, depth):
    bsz, seq, d = h.shape
    t = bsz * seq
    n_mem = mem.shape[1]
    alpha = (2.0 * depth) ** 0.25

    d_inner = w_branch_ssd.shape[0]
    xbc_w = ssd_conv_w.shape[1]
    n_heads = ssd_dt_bias.shape[0]
    lru_w = w_branch_lru.shape[0]
    mem_w = w_branch_mem.shape[0]
    n_branch = b_gate.shape[0]
    c_dt = d_inner + xbc_w
    c_lru = c_dt + n_heads
    c_gate = c_lru + 2 * lru_w + mem_w
    assert w_in.shape[1] == c_gate + n_branch * d and n_branch == 3

    hf = h.reshape(t, d)
    h_bf = hf.astype(BF16)
    assert lru_w == mem_w
    w_t = w_in.T
    w_xz = jnp.concatenate([w_t[d_inner:c_dt], w_t[:d_inner]], axis=0).astype(BF16)
    w_dt = jnp.pad(w_t[c_dt:c_lru], ((0, LANES - n_heads), (0, 0))).astype(BF16)
    w_lm = w_t[c_lru:c_gate].astype(BF16)
    w_g = w_t[c_gate:].astype(BF16)

    xz = _matmul(h_bf, w_xz, BF16, b_is_transposed=True, name="proj_xz")
    dt_pad = _matmul(h_bf, w_dt, F32, b_is_transposed=True, name="proj_dt")
    lm = _matmul(h_bf, w_lm, BF16, b_is_transposed=True, name="proj_lru_mem")
    gate_pre = _matmul(h_bf, w_g, BF16, b_is_transposed=True, name="proj_gates")

    y_ssd = _ssd_branch(xz, dt_pad, ssd_conv_w, ssd_conv_b, ssd_dt_bias,
                        ssd_a_log, ssd_d, ssd_norm_w, batch=bsz, seq=seq)
    y_lru = _rglru_branch(lm, lru_conv_w, lru_conv_b, lru_w_a, lru_b_a,
                          lru_w_i, lru_b_i, lru_lambda, batch=bsz, seq=seq)
    kv = _matmul(mem.reshape(bsz * n_mem, d).astype(BF16), mem_w_kv.astype(BF16), BF16, name="mem_kv")
    y_mem = _memory_xattn(lm, 2, kv, batch=bsz, seq=seq, n_mem=n_mem)

    merged = _merge(y_ssd, y_lru, y_mem, w_branch_ssd.astype(BF16), w_branch_lru.astype(BF16),
                    w_branch_mem.astype(BF16), gate_pre, b_gate.astype(F32))
    pre1 = _matmul(merged, w_out.astype(BF16), F32, tn=512, res=hf, res_scale=alpha, name="out_proj")
    h1, h1_bf = _layernorm(pre1, ln1_g, ln1_b, with_bf16=True)

    n_keys = peer_keys.shape[2]
    q = _matmul(h1_bf, peer_w_q.astype(BF16), BF16, name="peer_q")
    hi, lo, gate = _peer_route(q, peer_keys.astype(BF16))
    hi_t, lo_t, gate_t = hi.T, lo.T, gate.T
    w_t = _peer_act(h1_bf, peer_u.astype(BF16), hi_t, lo_t, gate_t)
    dense_w = _peer_scatter(hi_t, lo_t, w_t, n_keys)
    pre2 = _matmul(dense_w, peer_v.astype(BF16), F32, tm=512, tn=256, res=h1, res_scale=alpha, name="peer_v")
    (out,) = _layernorm(pre2, ln2_g, ln2_b, with_bf16=False)
    return out.reshape(bsz, seq, d)


def kernel(x, mem, w_in, b_gate, ssd_conv_w, ssd_conv_b, ssd_dt_bias, ssd_a_log, ssd_d, ssd_norm_w, lru_conv_w, lru_conv_b, lru_w_a, lru_b_a, lru_w_i, lru_b_i, lru_lambda, mem_w_kv, w_branch_ssd, w_branch_lru, w_branch_mem, w_out, ln1_g, ln1_b, peer_w_q, peer_keys, peer_u, peer_v, ln2_g, ln2_b):
    params = (w_in, b_gate, ssd_conv_w, ssd_conv_b, ssd_dt_bias, ssd_a_log, ssd_d, ssd_norm_w, lru_conv_w,
              lru_conv_b, lru_w_a, lru_b_a, lru_w_i, lru_b_i, lru_lambda, mem_w_kv, w_branch_ssd, w_branch_lru,
              w_branch_mem, w_out, ln1_g, ln1_b, peer_w_q, peer_keys, peer_u, peer_v, ln2_g, ln2_b)
    depth = w_in.shape[0]
    h = x
    for l in range(depth):
        h = _layer(h, mem, *(p[l] for p in params), depth=depth)
    return h
```
